```python
import jax, jax.numpy as jnp
from jax import lax
import numpy as np

D_MODEL = 1024
BATCH = 16
SEQ = 2048
DEPTH = 2

D_MIX = D_MODEL
HEAD_DIM = 64
ATTN_WIDTH = D_MIX // 2
N_ATTN_HEADS = ATTN_WIDTH // HEAD_DIM
GCONV_WIDTH = D_MIX // 4
CCONV_WIDTH = D_MIX - ATTN_WIDTH - GCONV_WIDTH
DILATED_PAIRS = ((128, 1), (512, 4), (2048, 16))
ATTN_BLOCK = 128
GCONV_K = 3
CCONV_K = 31
D_FF = 11 * D_MODEL // 4
IN_SPLITS = (ATTN_WIDTH, ATTN_WIDTH, ATTN_WIDTH,
             GCONV_WIDTH, GCONV_WIDTH, GCONV_WIDTH,
             CCONV_WIDTH, CCONV_WIDTH)
D_IN = sum(IN_SPLITS)
RMS_EPS = 1e-6
LN_EPS = 1e-5

kernel_name = "hybrid_dilated_attn_gconv_conformer_macaron"


def rms_norm(x, g):
    xf = x.astype(jnp.float32)
    y = xf * lax.rsqrt(jnp.mean(xf * xf, axis=-1, keepdims=True) + RMS_EPS)
    return (y * g.astype(jnp.float32)).astype(x.dtype)


def layer_norm(x, g, b):
    xf = x.astype(jnp.float32)
    mu = jnp.mean(xf, axis=-1, keepdims=True)
    var = jnp.mean(jnp.square(xf - mu), axis=-1, keepdims=True)
    y = (xf - mu) * lax.rsqrt(var + LN_EPS)
    return (y * g.astype(jnp.float32) + b.astype(jnp.float32)).astype(x.dtype)


def swiglu(h, wg, wu, wd):
    return (jax.nn.silu(h @ wg) * (h @ wu)) @ wd


def causal_depthwise_conv(u, w):
    kw, c = w.shape
    return lax.conv_general_dilated(
        u, w[:, None, :], window_strides=(1,), padding=[(kw - 1, 0)],
        dimension_numbers=('NWC', 'WIO', 'NWC'), feature_group_count=c)


def alibi_slopes(n_heads):
    return jnp.exp2(-8.0 * jnp.arange(1, n_heads + 1, dtype=jnp.float32) / n_heads)


def dilated_branch(q, k, v, dil, n_steps, slopes):
    b, s, h, e = q.shape
    blk = ATTN_BLOCK
    L = s // dil
    nb = -(-L // blk)
    Lp = nb * blk

    def strided(t):
        return t.reshape(b, L, dil, h, e).transpose(0, 2, 3, 1, 4)

    qb = jnp.pad(strided(q), ((0, 0), (0, 0), (0, 0), (0, Lp - L), (0, 0)))
    qb = qb.reshape(b, dil, h, nb, blk, e)

    def band(t):
        tp = jnp.pad(t, ((0, 0), (0, 0), (0, 0), (blk, Lp - L), (0, 0)))
        prev = tp[:, :, :, :Lp].reshape(b, dil, h, nb, blk, e)
        cur = tp[:, :, :, blk:].reshape(b, dil, h, nb, blk, e)
        return jnp.concatenate([prev, cur], axis=4)

    kb = band(strided(k))
    vb = band(strided(v))

    scores = jnp.einsum('brhnqe,brhnke->brhnqk', qb, kb).astype(jnp.float32) * (e ** -0.5)
    qi = jnp.arange(blk)[:, None]
    kj = jnp.arange(2 * blk)[None, :]
    steps = qi - kj + blk
    key_pos = jnp.arange(nb)[:, None, None] * blk - blk + kj[None]
    valid = (steps >= 0) & (steps <= n_steps) & (key_pos >= 0)
    bias = -slopes[:, None, None, None] * (steps * dil).astype(jnp.float32)
    scores = jnp.where(valid, scores + bias, -jnp.inf)
    lse = jax.nn.logsumexp(scores, axis=-1)
    p = jnp.exp(scores - lse[..., None]).astype(v.dtype)
    o = jnp.einsum('brhnqk,brhnke->brhnqe', p, vb)
    o = o.reshape(b, dil, h, Lp, e)[:, :, :, :L].transpose(0, 3, 1, 2, 4).reshape(b, s, h, e)
    lse = lse.reshape(b, dil, h, Lp)[..., :L].transpose(0, 3, 1, 2).reshape(b, s, h)
    return o, lse


def dilated_attention(q, k, v):
    slopes = alibi_slopes(q.shape[2])
    outs, lses = [], []
    for window, dil in DILATED_PAIRS:
        o, l = dilated_branch(q, k, v, dil, window // dil, slopes)
        outs.append(o)
        lses.append(l)
    wts = jax.nn.softmax(jnp.stack(lses, axis=0), axis=0)
    return jnp.sum(wts[..., None].astype(q.dtype) * jnp.stack(outs, axis=0), axis=0)


def hybrid_mixer(h, w_in, w_out, gconv_w, cconv_w, cconv_b, cln_g, cln_b):
    b, s, _ = h.shape
    z = h @ w_in
    q, k, v, g_b, g_c, g_x, c_val, c_gate = jnp.split(
        z, list(np.cumsum(IN_SPLITS)[:-1]), axis=-1)

    def heads(t):
        return t.reshape(b, s, N_ATTN_HEADS, HEAD_DIM)
    y_attn = dilated_attention(heads(q), heads(k), heads(v)).reshape(b, s, ATTN_WIDTH)

    y_gconv = g_b * causal_depthwise_conv(g_c * g_x, gconv_w)

    u = c_val * jax.nn.sigmoid(c_gate)
    u = causal_depthwise_conv(u, cconv_w) + cconv_b
    y_cconv = jax.nn.silu(layer_norm(u, cln_g, cln_b))

    return jnp.concatenate([y_attn, y_gconv, y_cconv], axis=-1) @ w_out


def setup_inputs(seed: int = 0) -> dict:
    key = jax.random.key(seed)
    ks = jax.random.split(key, 20)
    f32 = jnp.float32

    def nrm(k, shape, fan_in):
        return jax.random.normal(k, shape, f32) * (fan_in ** -0.5)

    def gain(k, shape):
        return 1.0 + 0.01 * jax.random.normal(k, shape, f32)

    return {
        "x": jax.random.normal(ks[0], (BATCH, SEQ, D_MODEL), f32),
        "w_in": nrm(ks[1], (DEPTH, D_MODEL, D_IN), D_MODEL),
        "w_out": nrm(ks[2], (DEPTH, D_MIX, D_MODEL), D_MIX),
        "gconv_w": nrm(ks[3], (DEPTH, GCONV_K, GCONV_WIDTH), GCONV_K),
        "cconv_w": nrm(ks[4], (DEPTH, CCONV_K, CCONV_WIDTH), CCONV_K),
        "cconv_b": 0.01 * jax.random.normal(ks[5], (DEPTH, CCONV_WIDTH), f32),
        "cln_g": gain(ks[6], (DEPTH, CCONV_WIDTH)),
        "cln_b": 0.01 * jax.random.normal(ks[7], (DEPTH, CCONV_WIDTH), f32),
        "ffn1_wg": nrm(ks[8], (DEPTH, D_MODEL, D_FF), D_MODEL),
        "ffn1_wu": nrm(ks[9], (DEPTH, D_MODEL, D_FF), D_MODEL),
        "ffn1_wd": nrm(ks[10], (DEPTH, D_FF, D_MODEL), D_FF),
        "ffn2_wg": nrm(ks[11], (DEPTH, D_MODEL, D_FF), D_MODEL),
        "ffn2_wu": nrm(ks[12], (DEPTH, D_MODEL, D_FF), D_MODEL),
        "ffn2_wd": nrm(ks[13], (DEPTH, D_FF, D_MODEL), D_FF),
        "norm_ffn1": gain(ks[14], (DEPTH, D_MODEL)),
        "norm_mix": gain(ks[15], (DEPTH, D_MODEL)),
        "norm_ffn2": gain(ks[16], (DEPTH, D_MODEL)),
        "norm_final": gain(ks[17], (D_MODEL,)),
    }


def reference(x, w_in, w_out, gconv_w, cconv_w, cconv_b, cln_g, cln_b,
              ffn1_wg, ffn1_wu, ffn1_wd, ffn2_wg, ffn2_wu, ffn2_wd,
              norm_ffn1, norm_mix, norm_ffn2, norm_final):
    for l in range(DEPTH):
        x = x + 0.5 * swiglu(rms_norm(x, norm_ffn1[l]), ffn1_wg[l], ffn1_wu[l], ffn1_wd[l])
        x = x + hybrid_mixer(rms_norm(x, norm_mix[l]), w_in[l], w_out[l], gconv_w[l],
                             cconv_w[l], cconv_b[l], cln_g[l], cln_b[l])
        x = x + 0.5 * swiglu(rms_norm(x, norm_ffn2[l]), ffn2_wg[l], ffn2_wu[l], ffn2_wd[l])
    return rms_norm(x, norm_final)
```

```python
import functools

import numpy as np
import jax
import jax.numpy as jnp
from jax import lax
from jax.experimental import pallas as pl
from jax.experimental.pallas import tpu as pltpu

D_MODEL = 1024
D_FF = 11 * D_MODEL // 4
HEAD_DIM = 64
ATTN_WIDTH = D_MODEL // 2
N_HEADS = ATTN_WIDTH // HEAD_DIM
GCONV_WIDTH = D_MODEL // 4
CCONV_WIDTH = D_MODEL // 4
GCONV_K = 3
CCONV_K = 31
QKV_WIDTH = 3 * ATTN_WIDTH
REST_WIDTH = 3 * GCONV_WIDTH + 2 * CCONV_WIDTH
D_IN = QKV_WIDTH + REST_WIDTH
DILATIONS = (1, 4, 16)
N_STEPS = 128
BLK = 128
LANES = 128
SUBLANES = 8
RMS_EPS = 1e-6
LN_EPS = 1e-5
CONV_PAD = 32
CONV_CHUNK = 64
COMBINE_CHUNK = 256
VMEM_LIMIT = 56 * 1024 * 1024

F32 = jnp.float32
BF16 = jnp.bfloat16


def _rms(x, g):
    return x * lax.rsqrt(jnp.mean(x * x, axis=-1, keepdims=True) + RMS_EPS) * g


def _params(n_axes):
    return pltpu.CompilerParams(
        dimension_semantics=("arbitrary",) * n_axes, vmem_limit_bytes=VMEM_LIMIT)


def _resident(shape):
    return pl.BlockSpec(shape, lambda *_: (0,) * len(shape), pipeline_mode=pl.Buffered(1))


def _ffn_kernel(x_ref, g_ref, wg_ref, wu_ref, wd_ref, *rest, final):
    o_ref = rest[-1]
    x = x_ref[...]
    h = _rms(x, g_ref[...]).astype(BF16)
    gate = jnp.dot(h, wg_ref[...], preferred_element_type=F32)
    up = jnp.dot(h, wu_ref[...], preferred_element_type=F32)
    act = (gate * jax.nn.sigmoid(gate) * up).astype(BF16)
    y = x + 0.5 * jnp.dot(act, wd_ref[...], preferred_element_type=F32)
    if final:
        y = _rms(y, rest[0][...])
    o_ref[...] = y


def _ffn(x, gain, wg, wu, wd, final_gain=None, tm=512):
    t = x.shape[0]
    row = pl.BlockSpec((tm, D_MODEL), lambda i: (i, 0))
    in_specs = [row, _resident((1, D_MODEL)), _resident((D_MODEL, D_FF)),
                _resident((D_MODEL, D_FF)), _resident((D_FF, D_MODEL))]
    args = [x, gain.reshape(1, D_MODEL), wg, wu, wd]
    if final_gain is not None:
        in_specs.append(_resident((1, D_MODEL)))
        args.append(final_gain.reshape(1, D_MODEL))
    return pl.pallas_call(
        functools.partial(_ffn_kernel, final=final_gain is not None),
        grid=(t // tm,), in_specs=in_specs, out_specs=row,
        out_shape=jax.ShapeDtypeStruct((t, D_MODEL), F32),
        compiler_params=_params(1), name="ffn")(*args)


def _inproj_kernel(x_ref, g_ref, w_ref, qkv_ref, rest_ref):
    h = _rms(x_ref[...], g_ref[...]).astype(BF16)
    qkv_ref[...] = jnp.dot(h, w_ref[:, :QKV_WIDTH], preferred_element_type=F32)
    rest_ref[...] = jnp.dot(h, w_ref[:, QKV_WIDTH:], preferred_element_type=F32)


def _inproj(x, gain, w_in, tm=512):
    t = x.shape[0]
    return pl.pallas_call(
        _inproj_kernel, grid=(t // tm,),
        in_specs=[pl.BlockSpec((tm, D_MODEL), lambda i: (i, 0)),
                  _resident((1, D_MODEL)), _resident((D_MODEL, D_IN))],
        out_specs=[pl.BlockSpec((tm, QKV_WIDTH), lambda i: (i, 0)),
                   pl.BlockSpec((tm, REST_WIDTH), lambda i: (i, 0))],
        out_shape=[jax.ShapeDtypeStruct((t, QKV_WIDTH), F32),
                   jax.ShapeDtypeStruct((t, REST_WIDTH), F32)],
        compiler_params=_params(1), name="inproj")(x, gain.reshape(1, D_MODEL), w_in)


def _alibi_bias_table():
    slopes = np.exp2(-8.0 * np.arange(1, N_HEADS + 1, dtype=np.float32) / N_HEADS)
    qi = np.arange(BLK)[:, None]
    kj = np.arange(2 * BLK)[None, :]
    steps = qi - kj + BLK
    valid = (steps >= 0) & (steps <= N_STEPS)
    tables = []
    for dil in DILATIONS:
        bias = -slopes[:, None, None] * (steps * dil).astype(np.float32)[None]
        tables.append(np.where(valid[None], bias, -np.inf).astype(np.float32))
    return np.stack(tables)


def _attn_kernel(q_ref, k_ref, v_ref, bias_ref, o_ref, acc_ref, max_ref, *, seq):
    lane = lax.broadcasted_iota(jnp.int32, (1, LANES), 1)
    head_mask = (lane < HEAD_DIM, lane >= HEAD_DIM)

    def block(bi, dil, start, has_prev):
        def rows(s0):
            return pl.ds(s0, BLK, stride=dil) if dil > 1 else pl.ds(s0, BLK)
        q = q_ref[rows(start), :] * (HEAD_DIM ** -0.5)
        k = k_ref[rows(start), :]
        v = v_ref[rows(start), :]
        if has_prev:
            prev = start - BLK * dil
            k = jnp.concatenate([k_ref[rows(prev), :], k], axis=0)
            v = jnp.concatenate([v_ref[rows(prev), :], v], axis=0)
        kb = k.astype(BF16)
        for h in range(2):
            qh = jnp.where(head_mask[h], q, 0.0).astype(BF16)
            s = lax.dot_general(qh, kb, (((1,), (1,)), ((), ())),
                                preferred_element_type=F32)
            if has_prev:
                s = s + bias_ref[bi, h]
            else:
                s = s + bias_ref[bi, h, :, BLK:]
            m = jnp.max(s, axis=1, keepdims=True)
            p = jnp.exp(s - m).astype(BF16)
            v_aug = jnp.where(head_mask[h], v, 1.0).astype(BF16)
            acc_ref[2 * bi + h, rows(start), :] = jnp.dot(
                p, v_aug, preferred_element_type=F32)
            max_ref[2 * bi + h, rows(start), :] = jnp.broadcast_to(m, (BLK, LANES))

    for bi, dil in enumerate(DILATIONS):
        n_blocks = seq // dil // BLK

        def residue_class(r, carry, bi=bi, dil=dil, n_blocks=n_blocks):
            block(bi, dil, r, False)
            if n_blocks > 1:
                def later(n, c):
                    block(bi, dil, n * (BLK * dil) + r, True)
                    return c
                lax.fori_loop(1, n_blocks, later, 0)
            return carry

        if dil == 1:
            residue_class(0, 0)
        else:
            lax.fori_loop(0, dil, residue_class, 0)

    def combine(c, carry):
        rows = pl.ds(pl.multiple_of(c * COMBINE_CHUNK, COMBINE_CHUNK), COMBINE_CHUNK)
        outs = []
        for h in range(2):
            ms = [max_ref[2 * bi + h, rows, :] for bi in range(len(DILATIONS))]
            m = jnp.maximum(jnp.maximum(ms[0], ms[1]), ms[2])
            tot = sum(jnp.exp(ms[bi] - m) * acc_ref[2 * bi + h, rows, :]
                      for bi in range(len(DILATIONS)))
            outs.append(tot / pltpu.roll(tot, HEAD_DIM, axis=1))
        o_ref[rows, :] = jnp.where(head_mask[0], outs[0], outs[1]).astype(o_ref.dtype)
        return carry

    lax.fori_loop(0, seq // COMBINE_CHUNK, combine, 0)


def _attention(qkv, bias, batch, seq):
    n_pairs = N_HEADS // 2
    n_slabs = 2 * len(DILATIONS)
    return pl.pallas_call(
        functools.partial(_attn_kernel, seq=seq),
        grid=(batch, n_pairs),
        in_specs=[pl.BlockSpec((seq, LANES), lambda b, p: (b, p)),
                  pl.BlockSpec((seq, LANES), lambda b, p: (b, n_pairs + p)),
                  pl.BlockSpec((seq, LANES), lambda b, p: (b, 2 * n_pairs + p)),
                  pl.BlockSpec((len(DILATIONS), 2, BLK, 2 * BLK), lambda b, p: (0, p, 0, 0))],
        out_specs=pl.BlockSpec((seq, LANES), lambda b, p: (b, p)),
        out_shape=jax.ShapeDtypeStruct((batch * seq, ATTN_WIDTH), BF16),
        scratch_shapes=[pltpu.VMEM((n_slabs, seq, LANES), F32),
                        pltpu.VMEM((n_slabs, seq, LANES), F32)],
        compiler_params=_params(2), name="dilated_attn")(qkv, qkv, qkv, bias)


def _causal_conv_chunk(pad_ref, w_ref, row0, n_taps):
    win_rows = CONV_CHUNK + CONV_PAD
    win = pad_ref[pl.ds(row0, win_rows), :]
    first = CONV_PAD - (n_taps - 1)
    acc = None
    for sub in range(SUBLANES):
        offsets = [d for d in range(first, CONV_PAD + 1) if d % SUBLANES == sub]
        if not offsets:
            continue
        shifted = win if sub == 0 else pltpu.roll(win, win_rows - sub, axis=0)
        for d in offsets:
            j = d - first
            term = w_ref[j:j + 1, :] * shifted[d - sub:d - sub + CONV_CHUNK]
            acc = term if acc is None else acc + term
    return acc


def _conv_kernel(z_ref, gw_ref, cw_ref, cb_ref, lg_ref, lb_ref, o_ref, pad_ref, *, seq):
    g0, g1, g2 = 0, GCONV_WIDTH, 2 * GCONV_WIDTH
    c0 = 3 * GCONV_WIDTH
    c1 = c0 + CCONV_WIDTH
    n_chunks = seq // CONV_CHUNK
    pad_ref[0:CONV_PAD, :] = jnp.zeros((CONV_PAD, GCONV_WIDTH), F32)

    pad_ref[CONV_PAD:, :] = z_ref[:, g1:g2] * z_ref[:, g2:c0]

    def gconv(c, carry):
        row0 = pl.multiple_of(c * CONV_CHUNK, CONV_CHUNK)
        acc = _causal_conv_chunk(pad_ref, gw_ref, row0, GCONV_K)
        rows = pl.ds(row0, CONV_CHUNK)
        o_ref[rows, 0:GCONV_WIDTH] = (z_ref[rows, g0:g1] * acc).astype(o_ref.dtype)
        return carry

    lax.fori_loop(0, n_chunks, gconv, 0)

    pad_ref[CONV_PAD:, :] = z_ref[:, c0:c1] * jax.nn.sigmoid(z_ref[:, c1:])

    def cconv(c, carry):
        row0 = pl.multiple_of(c * CONV_CHUNK, CONV_CHUNK)
        u = _causal_conv_chunk(pad_ref, cw_ref, row0, CCONV_K) + cb_ref[...]
        mu = jnp.mean(u, axis=-1, keepdims=True)
        d = u - mu
        var = jnp.mean(d * d, axis=-1, keepdims=True)
        y = d * lax.rsqrt(var + LN_EPS) * lg_ref[...] + lb_ref[...]
        o_ref[pl.ds(row0, CONV_CHUNK), GCONV_WIDTH:] = (
            y * jax.nn.sigmoid(y)).astype(o_ref.dtype)
        return carry

    lax.fori_loop(0, n_chunks, cconv, 0)


def _conv_mixers(rest, gconv_w, cconv_w, cconv_b, cln_g, cln_b, batch, seq):
    width = GCONV_WIDTH + CCONV_WIDTH
    return pl.pallas_call(
        functools.partial(_conv_kernel, seq=seq),
        grid=(batch,),
        in_specs=[pl.BlockSpec((seq, REST_WIDTH), lambda b: (b, 0)),
                  _resident((GCONV_K, GCONV_WIDTH)), _resident((CCONV_K, CCONV_WIDTH)),
                  _resident((1, CCONV_WIDTH)), _resident((1, CCONV_WIDTH)),
                  _resident((1, CCONV_WIDTH))],
        out_specs=pl.BlockSpec((seq, width), lambda b: (b, 0)),
        out_shape=jax.ShapeDtypeStruct((batch * seq, width), BF16),
        scratch_shapes=[pltpu.VMEM((seq + CONV_PAD, GCONV_WIDTH), F32)],
        compiler_params=_params(1), name="conv_mixers")(
            rest, gconv_w, cconv_w, cconv_b.reshape(1, -1), cln_g.reshape(1, -1),
            cln_b.reshape(1, -1))


def _outproj_kernel(x_ref, ya_ref, yc_ref, w_ref, o_ref):
    o_ref[...] = (x_ref[...]
                  + jnp.dot(ya_ref[...], w_ref[:ATTN_WIDTH, :], preferred_element_type=F32)
                  + jnp.dot(yc_ref[...], w_ref[ATTN_WIDTH:, :], preferred_element_type=F32))


def _outproj(x, y_attn, y_conv, w_out, tm=1024):
    t = x.shape[0]
    row = pl.BlockSpec((tm, D_MODEL), lambda i: (i, 0))
    half = pl.BlockSpec((tm, ATTN_WIDTH), lambda i: (i, 0))
    return pl.pallas_call(
        _outproj_kernel, grid=(t // tm,),
        in_specs=[row, half, half, _resident((D_MODEL, D_MODEL))],
        out_specs=row, out_shape=jax.ShapeDtypeStruct((t, D_MODEL), F32),
        compiler_params=_params(1), name="outproj")(x, y_attn, y_conv, w_out)


def kernel(x, w_in, w_out, gconv_w, cconv_w, cconv_b, cln_g, cln_b, ffn1_wg, ffn1_wu, ffn1_wd, ffn2_wg, ffn2_wu, ffn2_wd, norm_ffn1, norm_mix, norm_ffn2, norm_final):
    batch, seq, _ = x.shape
    depth = w_in.shape[0]
    assert seq % (BLK * max(DILATIONS)) == 0 and x.shape[2] == D_MODEL
    bias = jnp.asarray(_alibi_bias_table())
    h = x.reshape(batch * seq, D_MODEL)
    for l in range(depth):
        h = _ffn(h, norm_ffn1[l], ffn1_wg[l].astype(BF16), ffn1_wu[l].astype(BF16),
                 ffn1_wd[l].astype(BF16))
        qkv, rest = _inproj(h, norm_mix[l], w_in[l].astype(BF16))
        y_attn = _attention(qkv, bias, batch, seq)
        y_conv = _conv_mixers(rest, gconv_w[l], cconv_w[l], cconv_b[l], cln_g[l],
                              cln_b[l], batch, seq)
        h = _outproj(h, y_attn, y_conv, w_out[l].astype(BF16))
        h = _ffn(h, norm_ffn2[l], ffn2_wg[l].astype(BF16), ffn2_wu[l].astype(BF16),
                 ffn2_wd[l].astype(BF16),
                 final_gain=norm_final if l == depth - 1 else None)
    return h.reshape(batch, seq, D_MODEL)
```

```python
import functools
import math

import numpy as np
import jax
import jax.numpy as jnp
from jax import lax
from jax.experimental import pallas as pl
from jax.experimental.pallas import tpu as pltpu

D_MODEL = 1024
D_FF = 11 * D_MODEL // 4
HEAD_DIM = 64
ATTN_WIDTH = D_MODEL // 2
N_HEADS = ATTN_WIDTH // HEAD_DIM
GCONV_WIDTH = D_MODEL // 4
CCONV_WIDTH = D_MODEL // 4
GCONV_K = 3
CCONV_K = 31
QKV_WIDTH = 3 * ATTN_WIDTH
REST_WIDTH = 3 * GCONV_WIDTH + 2 * CCONV_WIDTH
D_IN = QKV_WIDTH + REST_WIDTH
DILATIONS = (1, 4, 16)
N_STEPS = 128
BLK = 128
N_CLASSES = max(DILATIONS)
LANES = 128
SUBLANES = 8
RMS_EPS = 1e-6
LN_EPS = 1e-5
LOG2E = math.log2(math.e)
ATTN_UNROLL = 4
COMBINE_CHUNK = 256
CONV_ROWS = 64
CONV_SHIFTS = (CCONV_K - 1) // N_CLASSES + 2
VMEM_LIMIT = 56 * 1024 * 1024

F32 = jnp.float32
BF16 = jnp.bfloat16


def _rms(x, g):
    return x * lax.rsqrt(jnp.mean(x * x, axis=-1, keepdims=True) + RMS_EPS) * g


def _params(n_axes):
    return pltpu.CompilerParams(
        dimension_semantics=("arbitrary",) * n_axes, vmem_limit_bytes=VMEM_LIMIT)


def _resident(shape):
    return pl.BlockSpec(shape, lambda *_: (0,) * len(shape), pipeline_mode=pl.Buffered(1))


def _copy_kernel(x_ref, o_ref):
    o_ref[...] = x_ref[...]


def _to_class_major(x):
    batch, seq, d = x.shape
    cls_rows = seq // N_CLASSES
    return pl.pallas_call(
        _copy_kernel, grid=(batch, N_CLASSES),
        in_specs=[pl.BlockSpec((None, cls_rows, d), lambda b, r: (b, 0, r))],
        out_specs=pl.BlockSpec((cls_rows, d), lambda b, r: (b * N_CLASSES + r, 0)),
        out_shape=jax.ShapeDtypeStruct((batch * seq, d), x.dtype),
        compiler_params=_params(2), name="to_class_major")(
            x.reshape(batch, cls_rows, N_CLASSES * d))


def _to_natural(x, batch, seq):
    d = x.shape[1]
    cls_rows = seq // N_CLASSES
    out = pl.pallas_call(
        _copy_kernel, grid=(batch, N_CLASSES),
        in_specs=[pl.BlockSpec((cls_rows, d), lambda b, r: (b * N_CLASSES + r, 0))],
        out_specs=pl.BlockSpec((None, cls_rows, d), lambda b, r: (b, 0, r)),
        out_shape=jax.ShapeDtypeStruct((batch, cls_rows, N_CLASSES * d), x.dtype),
        compiler_params=_params(2), name="to_natural")(x)
    return out.reshape(batch, seq, d)


def _ffn_kernel(x_ref, g_ref, wg_ref, wu_ref, wd_ref, *rest, final):
    o_ref = rest[-1]
    x = x_ref[...]
    h = _rms(x, g_ref[...]).astype(BF16)
    gate = jnp.dot(h, wg_ref[...], preferred_element_type=F32)
    up = jnp.dot(h, wu_ref[...], preferred_element_type=F32)
    act = (gate * jax.nn.sigmoid(gate) * up).astype(BF16)
    y = x + 0.5 * jnp.dot(act, wd_ref[...], preferred_element_type=F32)
    if final:
        y = _rms(y, rest[0][...])
    o_ref[...] = y


def _ffn(x, gain, wg, wu, wd, final_gain=None, tm=512):
    t = x.shape[0]
    row = pl.BlockSpec((tm, D_MODEL), lambda i: (i, 0))
    in_specs = [row, _resident((1, D_MODEL)), _resident((D_MODEL, D_FF)),
                _resident((D_MODEL, D_FF)), _resident((D_FF, D_MODEL))]
    args = [x, gain.reshape(1, D_MODEL), wg, wu, wd]
    if final_gain is not None:
        in_specs.append(_resident((1, D_MODEL)))
        args.append(final_gain.reshape(1, D_MODEL))
    return pl.pallas_call(
        functools.partial(_ffn_kernel, final=final_gain is not None),
        grid=(t // tm,), in_specs=in_specs, out_specs=row,
        out_shape=jax.ShapeDtypeStruct((t, D_MODEL), F32),
        compiler_params=_params(1), name="ffn")(*args)


def _inproj_kernel(x_ref, g_ref, w_ref, qkv_ref, rest_ref):
    h = _rms(x_ref[...], g_ref[...]).astype(BF16)
    qkv_ref[...] = jnp.dot(h, w_ref[:, :QKV_WIDTH], preferred_element_type=F32)
    rest_ref[...] = jnp.dot(h, w_ref[:, QKV_WIDTH:], preferred_element_type=F32)


def _inproj(x, gain, w_in, tm=512):
    t = x.shape[0]
    return pl.pallas_call(
        _inproj_kernel, grid=(t // tm,),
        in_specs=[pl.BlockSpec((tm, D_MODEL), lambda i: (i, 0)),
                  _resident((1, D_MODEL)), _resident((D_MODEL, D_IN))],
        out_specs=[pl.BlockSpec((tm, QKV_WIDTH), lambda i: (i, 0)),
                   pl.BlockSpec((tm, REST_WIDTH), lambda i: (i, 0))],
        out_shape=[jax.ShapeDtypeStruct((t, QKV_WIDTH), F32),
                   jax.ShapeDtypeStruct((t, REST_WIDTH), F32)],
        compiler_params=_params(1), name="inproj")(x, gain.reshape(1, D_MODEL), w_in)


def _alibi_bias_tables():
    slopes = np.exp2(-8.0 * np.arange(1, N_HEADS + 1, dtype=np.float64) / N_HEADS)
    with_prev, first = [], []
    for dil in DILATIONS:
        groups = N_CLASSES // dil
        piece = BLK // groups
        q_pos = (groups * np.arange(piece)[None, :] + np.arange(groups)[:, None]).reshape(-1)
        for table, key_piece, shift in ((with_prev, 2 * piece, BLK), (first, piece, 0)):
            k_pos = (groups * np.arange(key_piece)[None, :]
                     + np.arange(groups)[:, None]).reshape(-1) - shift
            steps = q_pos[:, None] - k_pos[None, :]
            valid = (steps >= 0) & (steps <= N_STEPS)
            bias = -slopes[:, None, None] * (steps * dil)[None] * LOG2E
            table.append(np.where(valid[None], bias, -np.inf).astype(np.float32))
    return np.stack(with_prev), np.stack(first)


def _attn_kernel(q_ref, k_ref, v_ref, bias_prev_ref, bias_first_ref, o_ref,
                 acc_ref, max_ref, s_ref, p_ref):
    lane = lax.broadcasted_iota(jnp.int32, (1, LANES), 1)
    head_mask = (lane < HEAD_DIM, lane >= HEAD_DIM)

    def pieces(dil, cls, n, with_prev):
        groups = N_CLASSES // dil
        piece = BLK // groups
        first = cls * BLK + (n - 1 if with_prev else n) * piece
        rows = 2 * piece if with_prev else piece
        if not isinstance(first, int):
            first = pl.multiple_of(first, SUBLANES)
        return [((dil * g) * BLK + first, rows) for g in range(groups)]

    def gather(ref, where):
        parts = [ref[pl.ds(start, rows), :] for start, rows in where]
        return parts[0] if len(parts) == 1 else jnp.concatenate(parts, axis=0)

    def scatter(ref, slab, where, value):
        offset = 0
        for start, rows in where:
            ref[slab, pl.ds(start, rows), :] = value[offset:offset + rows]
            offset += rows

    def run_blocks(bi, dil, blocks):
        where = [(pieces(dil, cls, n, False), pieces(dil, cls, n, has_prev), has_prev)
                 for cls, n, has_prev in blocks]
        for u, (q_rows, k_rows, has_prev) in enumerate(where):
            keys = 2 * BLK if has_prev else BLK
            q = gather(q_ref, q_rows) * (HEAD_DIM ** -0.5 * LOG2E)
            kb = gather(k_ref, k_rows).astype(BF16)
            for h in range(2):
                qh = jnp.where(head_mask[h], q, 0.0).astype(BF16)
                s = lax.dot_general(qh, kb, (((1,), (1,)), ((), ())),
                                    preferred_element_type=F32)
                bias = bias_prev_ref[bi, h] if has_prev else bias_first_ref[bi, h]
                s_ref[2 * u + h, :, :keys] = s + bias
        for u, (q_rows, k_rows, has_prev) in enumerate(where):
            keys = 2 * BLK if has_prev else BLK
            for h in range(2):
                s = s_ref[2 * u + h, :, :keys]
                m = jnp.max(s, axis=1, keepdims=True)
                p_ref[2 * u + h, :, :keys] = jnp.exp2(s - m).astype(BF16)
                scatter(max_ref, 2 * bi + h, q_rows, jnp.broadcast_to(m, (BLK, LANES)))
        for u, (q_rows, k_rows, has_prev) in enumerate(where):
            keys = 2 * BLK if has_prev else BLK
            v = gather(v_ref, k_rows)
            for h in range(2):
                v_aug = jnp.where(head_mask[h], v, 1.0).astype(BF16)
                scatter(acc_ref, 2 * bi + h, q_rows,
                        jnp.dot(p_ref[2 * u + h, :, :keys], v_aug,
                                preferred_element_type=F32))

    for bi, dil in enumerate(DILATIONS):
        n_blocks = N_CLASSES // dil
        if n_blocks == 1:
            def classes(grp, carry, bi=bi, dil=dil):
                run_blocks(bi, dil, [(grp * ATTN_UNROLL + u, 0, False)
                                     for u in range(ATTN_UNROLL)])
                return carry
            lax.fori_loop(0, dil // ATTN_UNROLL, classes, 0)
            continue

        def group(grp, cls, first, bi=bi, dil=dil):
            run_blocks(bi, dil, [(cls, grp * ATTN_UNROLL + u, not (first and u == 0))
                                 for u in range(ATTN_UNROLL)])

        def residue_class(cls, carry, n_blocks=n_blocks, group=group):
            group(0, cls, True)
            if n_blocks > ATTN_UNROLL:
                def later(grp, c):
                    group(grp, cls, False)
                    return c
                lax.fori_loop(1, n_blocks // ATTN_UNROLL, later, 0)
            return carry

        if dil == 1:
            residue_class(0, 0)
        else:
            lax.fori_loop(0, dil, residue_class, 0)

    def combine(c, carry):
        rows = pl.ds(pl.multiple_of(c * COMBINE_CHUNK, COMBINE_CHUNK), COMBINE_CHUNK)
        outs = []
        for h in range(2):
            ms = [max_ref[2 * bi + h, rows, :] for bi in range(len(DILATIONS))]
            m = jnp.maximum(jnp.maximum(ms[0], ms[1]), ms[2])
            tot = sum(jnp.exp2(ms[bi] - m) * acc_ref[2 * bi + h, rows, :]
                      for bi in range(len(DILATIONS)))
            outs.append(tot / pltpu.roll(tot, HEAD_DIM, axis=1))
        o_ref[rows, :] = jnp.where(head_mask[0], outs[0], outs[1]).astype(o_ref.dtype)
        return carry

    lax.fori_loop(0, o_ref.shape[0] // COMBINE_CHUNK, combine, 0)


def _attention(qkv, bias_prev, bias_first, batch, seq):
    n_pairs = N_HEADS // 2
    n_slabs = 2 * len(DILATIONS)
    n_dil = len(DILATIONS)
    return pl.pallas_call(
        _attn_kernel,
        grid=(batch, n_pairs),
        in_specs=[pl.BlockSpec((seq, LANES), lambda b, p: (b, p)),
                  pl.BlockSpec((seq, LANES), lambda b, p: (b, n_pairs + p)),
                  pl.BlockSpec((seq, LANES), lambda b, p: (b, 2 * n_pairs + p)),
                  pl.BlockSpec((n_dil, 2, BLK, 2 * BLK), lambda b, p: (0, p, 0, 0)),
                  pl.BlockSpec((n_dil, 2, BLK, BLK), lambda b, p: (0, p, 0, 0))],
        out_specs=pl.BlockSpec((seq, LANES), lambda b, p: (b, p)),
        out_shape=jax.ShapeDtypeStruct((batch * seq, ATTN_WIDTH), BF16),
        scratch_shapes=[pltpu.VMEM((n_slabs, seq, LANES), F32),
                        pltpu.VMEM((n_slabs, seq, LANES), F32),
                        pltpu.VMEM((2 * ATTN_UNROLL, BLK, 2 * BLK), F32),
                        pltpu.VMEM((2 * ATTN_UNROLL, BLK, 2 * BLK), BF16)],
        compiler_params=_params(2), name="dilated_attn")(qkv, qkv, qkv, bias_prev, bias_first)


def _conv_kernel(z_ref, gw_ref, cw_ref, cb_ref, lg_ref, lb_ref, o_ref, u_ref):
    cls_rows = z_ref.shape[0] // N_CLASSES
    g0, g1, g2 = 0, GCONV_WIDTH, 2 * GCONV_WIDTH
    c0 = 3 * GCONV_WIDTH
    c1 = c0 + CCONV_WIDTH
    row = lax.broadcasted_iota(jnp.int32, (cls_rows, 1), 0)

    def fill(conv_input, n_shifts):
        def body(r, carry):
            rows = pl.ds(pl.multiple_of(r * cls_rows, cls_rows), cls_rows)
            u = conv_input(rows)
            u_ref[0, rows, :] = u
            for d in range(1, n_shifts):
                u_ref[d, rows, :] = jnp.where(row >= d, pltpu.roll(u, d, axis=0), 0.0)
            return carry
        lax.fori_loop(0, N_CLASSES, body, 0)

    def conv_tile(w_ref, n_taps, r, row0):
        acc = None
        for s in range(n_taps):
            whole, part = divmod(s, N_CLASSES)
            borrow = (r < part).astype(jnp.int32) if part else 0
            src_cls = r - part + N_CLASSES * borrow
            start = pl.multiple_of(src_cls * cls_rows + row0, CONV_ROWS)
            src = u_ref[whole + borrow, pl.ds(start, CONV_ROWS), :]
            term = w_ref[n_taps - 1 - s:n_taps - s, :] * src
            acc = term if acc is None else acc + term
        return acc

    def tiles(body):
        def step(i, carry):
            r = i // (cls_rows // CONV_ROWS)
            row0 = (i % (cls_rows // CONV_ROWS)) * CONV_ROWS
            body(r, row0, pl.ds(pl.multiple_of(r * cls_rows + row0, CONV_ROWS), CONV_ROWS))
            return carry
        lax.fori_loop(0, N_CLASSES * (cls_rows // CONV_ROWS), step, 0)

    fill(lambda rows: z_ref[rows, g1:g2] * z_ref[rows, g2:c0],
         (GCONV_K - 1) // N_CLASSES + 2)

    def gconv(r, row0, rows):
        acc = conv_tile(gw_ref, GCONV_K, r, row0)
        o_ref[rows, 0:GCONV_WIDTH] = (z_ref[rows, g0:g1] * acc).astype(o_ref.dtype)

    tiles(gconv)

    fill(lambda rows: z_ref[rows, c0:c1] * jax.nn.sigmoid(z_ref[rows, c1:]), CONV_SHIFTS)

    def cconv(r, row0, rows):
        u = conv_tile(cw_ref, CCONV_K, r, row0) + cb_ref[...]
        mu = jnp.mean(u, axis=-1, keepdims=True)
        d = u - mu
        var = jnp.mean(d * d, axis=-1, keepdims=True)
        y = d * lax.rsqrt(var + LN_EPS) * lg_ref[...] + lb_ref[...]
        o_ref[rows, GCONV_WIDTH:] = (y * jax.nn.sigmoid(y)).astype(o_ref.dtype)

    tiles(cconv)


def _conv_mixers(rest, gconv_w, cconv_w, cconv_b, cln_g, cln_b, batch, seq):
    width = GCONV_WIDTH + CCONV_WIDTH
    return pl.pallas_call(
        _conv_kernel,
        grid=(batch,),
        in_specs=[pl.BlockSpec((seq, REST_WIDTH), lambda b: (b, 0)),
                  _resident((GCONV_K, GCONV_WIDTH)), _resident((CCONV_K, CCONV_WIDTH)),
                  _resident((1, CCONV_WIDTH)), _resident((1, CCONV_WIDTH)),
                  _resident((1, CCONV_WIDTH))],
        out_specs=pl.BlockSpec((seq, width), lambda b: (b, 0)),
        out_shape=jax.ShapeDtypeStruct((batch * seq, width), BF16),
        scratch_shapes=[pltpu.VMEM((CONV_SHIFTS, seq, GCONV_WIDTH), F32)],
        compiler_params=_params(1), name="conv_mixers")(
            rest, gconv_w, cconv_w, cconv_b.reshape(1, -1), cln_g.reshape(1, -1),
            cln_b.reshape(1, -1))


def _outproj_kernel(x_ref, ya_ref, yc_ref, w_ref, o_ref):
    o_ref[...] = (x_ref[...]
                  + jnp.dot(ya_ref[...], w_ref[:ATTN_WIDTH, :], preferred_element_type=F32)
                  + jnp.dot(yc_ref[...], w_ref[ATTN_WIDTH:, :], preferred_element_type=F32))


def _outproj(x, y_attn, y_conv, w_out, tm=1024):
    t = x.shape[0]
    row = pl.BlockSpec((tm, D_MODEL), lambda i: (i, 0))
    half = pl.BlockSpec((tm, ATTN_WIDTH), lambda i: (i, 0))
    return pl.pallas_call(
        _outproj_kernel, grid=(t // tm,),
        in_specs=[row, half, half, _resident((D_MODEL, D_MODEL))],
        out_specs=row, out_shape=jax.ShapeDtypeStruct((t, D_MODEL), F32),
        compiler_params=_params(1), name="outproj")(x, y_attn, y_conv, w_out)


def kernel(x, w_in, w_out, gconv_w, cconv_w, cconv_b, cln_g, cln_b, ffn1_wg, ffn1_wu, ffn1_wd, ffn2_wg, ffn2_wu, ffn2_wd, norm_ffn1, norm_mix, norm_ffn2, norm_final):
    batch, seq, _ = x.shape
    depth = w_in.shape[0]
    assert seq == N_CLASSES * BLK and x.shape[2] == D_MODEL
    bias_prev, bias_first = (jnp.asarray(t) for t in _alibi_bias_tables())
    h = _to_class_major(x)
    for l in range(depth):
        h = _ffn(h, norm_ffn1[l], ffn1_wg[l].astype(BF16), ffn1_wu[l].astype(BF16),
                 ffn1_wd[l].astype(BF16))
        qkv, rest = _inproj(h, norm_mix[l], w_in[l].astype(BF16))
        y_attn = _attention(qkv, bias_prev, bias_first, batch, seq)
        y_conv = _conv_mixers(rest, gconv_w[l], cconv_w[l], cconv_b[l], cln_g[l],
                              cln_b[l], batch, seq)
        h = _outproj(h, y_attn, y_conv, w_out[l].astype(BF16))
        h = _ffn(h, norm_ffn2[l], ffn2_wg[l].astype(BF16), ffn2_wu[l].astype(BF16),
                 ffn2_wd[l].astype(BF16),
                 final_gain=norm_final if l == depth - 1 else None)
    return _to_natural(h, batch, seq)
```

```python
import functools
import math

import numpy as np
import jax
import jax.numpy as jnp
from jax import lax
from jax.experimental import pallas as pl
from jax.experimental.pallas import tpu as pltpu

D_MODEL = 1024
D_FF = 11 * D_MODEL // 4
HEAD_DIM = 64
ATTN_WIDTH = D_MODEL // 2
N_HEADS = ATTN_WIDTH // HEAD_DIM
GCONV_WIDTH = D_MODEL // 4
CCONV_WIDTH = D_MODEL // 4
GCONV_K = 3
CCONV_K = 31
QKV_WIDTH = 3 * ATTN_WIDTH
REST_WIDTH = 3 * GCONV_WIDTH + 2 * CCONV_WIDTH
D_IN = QKV_WIDTH + REST_WIDTH
DILATIONS = (1, 4, 16)
N_STEPS = 128
BLK = 128
N_CLASSES = max(DILATIONS)
LANES = 128
SUBLANES = 8
CLS_PITCH = BLK + SUBLANES
RMS_EPS = 1e-6
LN_EPS = 1e-5
LOG2E = math.log2(math.e)
ATTN_UNROLL = 4
COMBINE_CHUNK = 256
CONV_ROWS = 64
CONV_SHIFTS = (CCONV_K - 1) // N_CLASSES + 2
VMEM_LIMIT = 56 * 1024 * 1024

F32 = jnp.float32
BF16 = jnp.bfloat16


def _rms(x, g):
    return x * lax.rsqrt(jnp.mean(x * x, axis=-1, keepdims=True) + RMS_EPS) * g


def _params(n_axes):
    return pltpu.CompilerParams(
        dimension_semantics=("arbitrary",) * n_axes, vmem_limit_bytes=VMEM_LIMIT)


def _resident(shape):
    return pl.BlockSpec(shape, lambda *_: (0,) * len(shape), pipeline_mode=pl.Buffered(1))


def _to_class_major_kernel(*refs):
    x_refs, o_ref = refs[:-1], refs[-1]
    cls_rows = o_ref.shape[0] // N_CLASSES

    def one_class(r, carry):
        rows = pl.ds(pl.multiple_of(r * cls_rows, cls_rows), cls_rows)
        for c, x_ref in enumerate(x_refs):
            o_ref[rows, c * LANES:(c + 1) * LANES] = (
                x_ref[pl.ds(r, cls_rows, stride=N_CLASSES), :])
        return carry

    lax.fori_loop(0, N_CLASSES, one_class, 0)


def _to_class_major(x):
    batch, seq, d = x.shape
    slabs = d // LANES
    return pl.pallas_call(
        _to_class_major_kernel, grid=(batch,),
        in_specs=[pl.BlockSpec((None, seq, LANES), lambda b, c=c: (b, 0, c))
                  for c in range(slabs)],
        out_specs=pl.BlockSpec((seq, d), lambda b: (b, 0)),
        out_shape=jax.ShapeDtypeStruct((batch * seq, d), x.dtype),
        compiler_params=_params(1), name="to_class_major")(*([x] * slabs))


def _to_natural_kernel(*refs):
    x_refs, o_ref, stage_ref = refs[:-2], refs[-2], refs[-1]
    cls_rows = o_ref.shape[0] // N_CLASSES
    for c, x_ref in enumerate(x_refs):
        for r in range(N_CLASSES):
            stage_ref[c, r * CLS_PITCH:r * CLS_PITCH + cls_rows, :] = (
                x_ref[r * cls_rows:(r + 1) * cls_rows, :])

        def some_tokens(i, carry, c=c):
            for j in range(SUBLANES):
                l = i * SUBLANES + j
                rows = pl.ds(pl.multiple_of(l * N_CLASSES, N_CLASSES), N_CLASSES)
                o_ref[rows, c * LANES:(c + 1) * LANES] = (
                    stage_ref[c, pl.ds(l, N_CLASSES, stride=CLS_PITCH), :])
            return carry

        lax.fori_loop(0, cls_rows // SUBLANES, some_tokens, 0)


def _to_natural(x, batch, seq):
    d = x.shape[1]
    slabs = d // LANES
    return pl.pallas_call(
        _to_natural_kernel, grid=(batch,),
        in_specs=[pl.BlockSpec((seq, LANES), lambda b, c=c: (b, c)) for c in range(slabs)],
        out_specs=pl.BlockSpec((None, seq, d), lambda b: (b, 0, 0)),
        out_shape=jax.ShapeDtypeStruct((batch, seq, d), x.dtype),
        scratch_shapes=[pltpu.VMEM((slabs, N_CLASSES * CLS_PITCH, LANES), x.dtype)],
        compiler_params=_params(1), name="to_natural")(*([x] * slabs))


def _ffn_kernel(x_ref, g_ref, wg_ref, wu_ref, wd_ref, *rest, final):
    o_ref = rest[-1]
    x = x_ref[...]
    h = _rms(x, g_ref[...]).astype(BF16)
    gate = jnp.dot(h, wg_ref[...], preferred_element_type=F32)
    up = jnp.dot(h, wu_ref[...], preferred_element_type=F32)
    act = (gate * jax.nn.sigmoid(gate) * up).astype(BF16)
    y = x + 0.5 * jnp.dot(act, wd_ref[...], preferred_element_type=F32)
    if final:
        y = _rms(y, rest[0][...])
    o_ref[...] = y


def _ffn(x, gain, wg, wu, wd, final_gain=None, tm=512):
    t = x.shape[0]
    row = pl.BlockSpec((tm, D_MODEL), lambda i: (i, 0))
    in_specs = [row, _resident((1, D_MODEL)), _resident((D_MODEL, D_FF)),
                _resident((D_MODEL, D_FF)), _resident((D_FF, D_MODEL))]
    args = [x, gain.reshape(1, D_MODEL), wg, wu, wd]
    if final_gain is not None:
        in_specs.append(_resident((1, D_MODEL)))
        args.append(final_gain.reshape(1, D_MODEL))
    return pl.pallas_call(
        functools.partial(_ffn_kernel, final=final_gain is not None),
        grid=(t // tm,), in_specs=in_specs, out_specs=row,
        out_shape=jax.ShapeDtypeStruct((t, D_MODEL), F32),
        compiler_params=_params(1), name="ffn")(*args)


def _inproj_kernel(x_ref, g_ref, w_ref, qkv_ref, rest_ref):
    h = _rms(x_ref[...], g_ref[...]).astype(BF16)
    qkv_ref[...] = jnp.dot(h, w_ref[:, :QKV_WIDTH], preferred_element_type=F32)
    rest_ref[...] = jnp.dot(h, w_ref[:, QKV_WIDTH:], preferred_element_type=F32)


def _inproj(x, gain, w_in, tm=512):
    t = x.shape[0]
    return pl.pallas_call(
        _inproj_kernel, grid=(t // tm,),
        in_specs=[pl.BlockSpec((tm, D_MODEL), lambda i: (i, 0)),
                  _resident((1, D_MODEL)), _resident((D_MODEL, D_IN))],
        out_specs=[pl.BlockSpec((tm, QKV_WIDTH), lambda i: (i, 0)),
                   pl.BlockSpec((tm, REST_WIDTH), lambda i: (i, 0))],
        out_shape=[jax.ShapeDtypeStruct((t, QKV_WIDTH), F32),
                   jax.ShapeDtypeStruct((t, REST_WIDTH), F32)],
        compiler_params=_params(1), name="inproj")(x, gain.reshape(1, D_MODEL), w_in)


def _alibi_bias_tables():
    slopes = np.exp2(-8.0 * np.arange(1, N_HEADS + 1, dtype=np.float64) / N_HEADS)
    with_prev, first = [], []
    for dil in DILATIONS:
        groups = N_CLASSES // dil
        piece = BLK // groups
        q_pos = (groups * np.arange(piece)[None, :] + np.arange(groups)[:, None]).reshape(-1)
        for table, key_piece, shift in ((with_prev, 2 * piece, BLK), (first, piece, 0)):
            k_pos = (groups * np.arange(key_piece)[None, :]
                     + np.arange(groups)[:, None]).reshape(-1) - shift
            steps = q_pos[:, None] - k_pos[None, :]
            valid = (steps >= 0) & (steps <= N_STEPS)
            bias = -slopes[:, None, None] * (steps * dil)[None] * LOG2E
            table.append(np.where(valid[None], bias, -np.inf).astype(np.float32))
    return np.stack(with_prev), np.stack(first)


def _attn_kernel(q_ref, k_ref, v_ref, bias_prev_ref, bias_first_ref, o_ref,
                 acc_ref, max_ref, s_ref, p_ref):
    lane = lax.broadcasted_iota(jnp.int32, (1, LANES), 1)
    head_mask = (lane < HEAD_DIM, lane >= HEAD_DIM)

    def pieces(dil, cls, n, with_prev):
        groups = N_CLASSES // dil
        piece = BLK // groups
        first = cls * BLK + (n - 1 if with_prev else n) * piece
        rows = 2 * piece if with_prev else piece
        if not isinstance(first, int):
            first = pl.multiple_of(first, SUBLANES)
        return [((dil * g) * BLK + first, rows) for g in range(groups)]

    def gather(ref, where):
        parts = [ref[pl.ds(start, rows), :] for start, rows in where]
        return parts[0] if len(parts) == 1 else jnp.concatenate(parts, axis=0)

    def scatter(ref, slab, where, value):
        offset = 0
        for start, rows in where:
            ref[slab, pl.ds(start, rows), :] = value[offset:offset + rows]
            offset += rows

    def run_blocks(bi, dil, blocks):
        where = [(pieces(dil, cls, n, False), pieces(dil, cls, n, has_prev), has_prev)
                 for cls, n, has_prev in blocks]
        for u, (q_rows, k_rows, has_prev) in enumerate(where):
            keys = 2 * BLK if has_prev else BLK
            q = gather(q_ref, q_rows) * (HEAD_DIM ** -0.5 * LOG2E)
            kb = gather(k_ref, k_rows).astype(BF16)
            for h in range(2):
                qh = jnp.where(head_mask[h], q, 0.0).astype(BF16)
                s = lax.dot_general(qh, kb, (((1,), (1,)), ((), ())),
                                    preferred_element_type=F32)
                bias = bias_prev_ref[bi, h] if has_prev else bias_first_ref[bi, h]
                s_ref[2 * u + h, :, :keys] = s + bias
        for u, (q_rows, k_rows, has_prev) in enumerate(where):
            keys = 2 * BLK if has_prev else BLK
            for h in range(2):
                s = s_ref[2 * u + h, :, :keys]
                m = jnp.max(s, axis=1, keepdims=True)
                p_ref[2 * u + h, :, :keys] = jnp.exp2(s - m).astype(BF16)
                scatter(max_ref, 2 * bi + h, q_rows, jnp.broadcast_to(m, (BLK, LANES)))
        for u, (q_rows, k_rows, has_prev) in enumerate(where):
            keys = 2 * BLK if has_prev else BLK
            v = gather(v_ref, k_rows)
            for h in range(2):
                v_aug = jnp.where(head_mask[h], v, 1.0).astype(BF16)
                scatter(acc_ref, 2 * bi + h, q_rows,
                        jnp.dot(p_ref[2 * u + h, :, :keys], v_aug,
                                preferred_element_type=F32))

    for bi, dil in enumerate(DILATIONS):
        n_blocks = N_CLASSES // dil
        if n_blocks == 1:
            def classes(grp, carry, bi=bi, dil=dil):
                run_blocks(bi, dil, [(grp * ATTN_UNROLL + u, 0, False)
                                     for u in range(ATTN_UNROLL)])
                return carry
            lax.fori_loop(0, dil // ATTN_UNROLL, classes, 0)
            continue

        def group(grp, cls, first, bi=bi, dil=dil):
            run_blocks(bi, dil, [(cls, grp * ATTN_UNROLL + u, not (first and u == 0))
                                 for u in range(ATTN_UNROLL)])

        def residue_class(cls, carry, n_blocks=n_blocks, group=group):
            group(0, cls, True)
            if n_blocks > ATTN_UNROLL:
                def later(grp, c):
                    group(grp, cls, False)
                    return c
                lax.fori_loop(1, n_blocks // ATTN_UNROLL, later, 0)
            return carry

        if dil == 1:
            residue_class(0, 0)
        else:
            lax.fori_loop(0, dil, residue_class, 0)

    def combine(c, carry):
        rows = pl.ds(pl.multiple_of(c * COMBINE_CHUNK, COMBINE_CHUNK), COMBINE_CHUNK)
        outs = []
        for h in range(2):
            ms = [max_ref[2 * bi + h, rows, :] for bi in range(len(DILATIONS))]
            m = jnp.maximum(jnp.maximum(ms[0], ms[1]), ms[2])
            tot = sum(jnp.exp2(ms[bi] - m) * acc_ref[2 * bi + h, rows, :]
                      for bi in range(len(DILATIONS)))
            outs.append(tot / pltpu.roll(tot, HEAD_DIM, axis=1))
        o_ref[rows, :] = jnp.where(head_mask[0], outs[0], outs[1]).astype(o_ref.dtype)
        return carry

    lax.fori_loop(0, o_ref.shape[0] // COMBINE_CHUNK, combine, 0)


def _attention(qkv, bias_prev, bias_first, batch, seq):
    n_pairs = N_HEADS // 2
    n_slabs = 2 * len(DILATIONS)
    n_dil = len(DILATIONS)
    return pl.pallas_call(
        _attn_kernel,
        grid=(batch, n_pairs),
        in_specs=[pl.BlockSpec((seq, LANES), lambda b, p: (b, p)),
                  pl.BlockSpec((seq, LANES), lambda b, p: (b, n_pairs + p)),
                  pl.BlockSpec((seq, LANES), lambda b, p: (b, 2 * n_pairs + p)),
                  pl.BlockSpec((n_dil, 2, BLK, 2 * BLK), lambda b, p: (0, p, 0, 0)),
                  pl.BlockSpec((n_dil, 2, BLK, BLK), lambda b, p: (0, p, 0, 0))],
        out_specs=pl.BlockSpec((seq, LANES), lambda b, p: (b, p)),
        out_shape=jax.ShapeDtypeStruct((batch * seq, ATTN_WIDTH), BF16),
        scratch_shapes=[pltpu.VMEM((n_slabs, seq, LANES), F32),
                        pltpu.VMEM((n_slabs, seq, LANES), F32),
                        pltpu.VMEM((2 * ATTN_UNROLL, BLK, 2 * BLK), F32),
                        pltpu.VMEM((2 * ATTN_UNROLL, BLK, 2 * BLK), BF16)],
        compiler_params=_params(2), name="dilated_attn")(qkv, qkv, qkv, bias_prev, bias_first)


def _conv_kernel(z_ref, gw_ref, cw_ref, cb_ref, lg_ref, lb_ref, o_ref, u_ref, pre_ref):
    cls_rows = z_ref.shape[0] // N_CLASSES
    g0, g1, g2 = 0, GCONV_WIDTH, 2 * GCONV_WIDTH
    c0 = 3 * GCONV_WIDTH
    c1 = c0 + CCONV_WIDTH
    row = lax.broadcasted_iota(jnp.int32, (cls_rows, 1), 0)

    def fill(conv_input, n_shifts):
        def body(r, carry):
            rows = pl.ds(pl.multiple_of(r * cls_rows, cls_rows), cls_rows)
            u = conv_input(rows)
            u_ref[0, rows, :] = u
            for d in range(1, n_shifts):
                u_ref[d, rows, :] = jnp.where(row >= d, pltpu.roll(u, d, axis=0), 0.0)
            return carry
        lax.fori_loop(0, N_CLASSES, body, 0)

    def conv_tile(w_ref, n_taps, r, row0):
        acc = None
        for s in range(n_taps):
            whole, part = divmod(s, N_CLASSES)
            borrow = jnp.where(r < part, 1, 0) if part else 0
            src_cls = r - part + N_CLASSES * borrow
            start = pl.multiple_of(src_cls * cls_rows + row0, CONV_ROWS)
            src = u_ref[whole + borrow, pl.ds(start, CONV_ROWS), :]
            term = w_ref[n_taps - 1 - s:n_taps - s, :] * src
            acc = term if acc is None else acc + term
        return acc

    def tiles(body):
        def step(i, carry):
            r = i // (cls_rows // CONV_ROWS)
            row0 = (i % (cls_rows // CONV_ROWS)) * CONV_ROWS
            body(r, row0, pl.ds(pl.multiple_of(r * cls_rows + row0, CONV_ROWS), CONV_ROWS))
            return carry
        lax.fori_loop(0, N_CLASSES * (cls_rows // CONV_ROWS), step, 0)

    fill(lambda rows: z_ref[rows, g1:g2] * z_ref[rows, g2:c0],
         (GCONV_K - 1) // N_CLASSES + 2)

    def gconv(r, row0, rows):
        acc = conv_tile(gw_ref, GCONV_K, r, row0)
        o_ref[rows, 0:GCONV_WIDTH] = (z_ref[rows, g0:g1] * acc).astype(o_ref.dtype)

    tiles(gconv)

    fill(lambda rows: z_ref[rows, c0:c1] * jax.nn.sigmoid(z_ref[rows, c1:]), CONV_SHIFTS)

    def conv_class(r):
        for row0 in range(0, cls_rows, CONV_ROWS):
            rows = pl.ds(pl.multiple_of(r * cls_rows + row0, CONV_ROWS), CONV_ROWS)
            pre_ref[rows, :] = conv_tile(cw_ref, CCONV_K, r, row0) + cb_ref[...]

    def norm_class(r):
        rows = pl.ds(pl.multiple_of(r * cls_rows, cls_rows), cls_rows)
        u = pre_ref[rows, :]
        mu = jnp.mean(u, axis=-1, keepdims=True)
        d = u - mu
        var = jnp.mean(d * d, axis=-1, keepdims=True)
        y = d * lax.rsqrt(var + LN_EPS) * lg_ref[...] + lb_ref[...]
        o_ref[rows, GCONV_WIDTH:] = (y * jax.nn.sigmoid(y)).astype(o_ref.dtype)

    conv_class(0)

    def step(r, carry):
        norm_class(r - 1)
        conv_class(r)
        return carry

    lax.fori_loop(1, N_CLASSES, step, 0)
    norm_class(N_CLASSES - 1)


def _conv_mixers(rest, gconv_w, cconv_w, cconv_b, cln_g, cln_b, batch, seq):
    width = GCONV_WIDTH + CCONV_WIDTH
    return pl.pallas_call(
        _conv_kernel,
        grid=(batch,),
        in_specs=[pl.BlockSpec((seq, REST_WIDTH), lambda b: (b, 0)),
                  _resident((GCONV_K, GCONV_WIDTH)), _resident((CCONV_K, CCONV_WIDTH)),
                  _resident((1, CCONV_WIDTH)), _resident((1, CCONV_WIDTH)),
                  _resident((1, CCONV_WIDTH))],
        out_specs=pl.BlockSpec((seq, width), lambda b: (b, 0)),
        out_shape=jax.ShapeDtypeStruct((batch * seq, width), BF16),
        scratch_shapes=[pltpu.VMEM((CONV_SHIFTS, seq, GCONV_WIDTH), F32),
                        pltpu.VMEM((seq, CCONV_WIDTH), F32)],
        compiler_params=_params(1), name="conv_mixers")(
            rest, gconv_w, cconv_w, cconv_b.reshape(1, -1), cln_g.reshape(1, -1),
            cln_b.reshape(1, -1))


def _outproj_kernel(x_ref, ya_ref, yc_ref, w_ref, o_ref):
    o_ref[...] = (x_ref[...]
                  + jnp.dot(ya_ref[...], w_ref[:ATTN_WIDTH, :], preferred_element_type=F32)
                  + jnp.dot(yc_ref[...], w_ref[ATTN_WIDTH:, :], preferred_element_type=F32))


def _outproj(x, y_attn, y_conv, w_out, tm=1024):
    t = x.shape[0]
    row = pl.BlockSpec((tm, D_MODEL), lambda i: (i, 0))
    half = pl.BlockSpec((tm, ATTN_WIDTH), lambda i: (i, 0))
    return pl.pallas_call(
        _outproj_kernel, grid=(t // tm,),
        in_specs=[row, half, half, _resident((D_MODEL, D_MODEL))],
        out_specs=row, out_shape=jax.ShapeDtypeStruct((t, D_MODEL), F32),
        compiler_params=_params(1), name="outproj")(x, y_attn, y_conv, w_out)


def kernel(x, w_in, w_out, gconv_w, cconv_w, cconv_b, cln_g, cln_b, ffn1_wg, ffn1_wu, ffn1_wd, ffn2_wg, ffn2_wu, ffn2_wd, norm_ffn1, norm_mix, norm_ffn2, norm_final):
    batch, seq, _ = x.shape
    depth = w_in.shape[0]
    assert seq == N_CLASSES * BLK and x.shape[2] == D_MODEL
    bias_prev, bias_first = (jnp.asarray(t) for t in _alibi_bias_tables())
    h = _to_class_major(x)
    for l in range(depth):
        h = _ffn(h, norm_ffn1[l], ffn1_wg[l].astype(BF16), ffn1_wu[l].astype(BF16),
                 ffn1_wd[l].astype(BF16))
        qkv, rest = _inproj(h, norm_mix[l], w_in[l].astype(BF16))
        y_attn = _attention(qkv, bias_prev, bias_first, batch, seq)
        y_conv = _conv_mixers(rest, gconv_w[l], cconv_w[l], cconv_b[l], cln_g[l],
                              cln_b[l], batch, seq)
        h = _outproj(h, y_attn, y_conv, w_out[l].astype(BF16))
        h = _ffn(h, norm_ffn2[l], ffn2_wg[l].astype(BF16), ffn2_wu[l].astype(BF16),
                 ffn2_wd[l].astype(BF16),
                 final_gain=norm_final if l == depth - 1 else None)
    return _to_natural(h, batch, seq)
```

```python
import functools
import math

import numpy as np
import jax
import jax.numpy as jnp
from jax import lax
from jax.experimental import pallas as pl
from jax.experimental.pallas import tpu as pltpu

D_MODEL = 1024
D_FF = 11 * D_MODEL // 4
HEAD_DIM = 64
ATTN_WIDTH = D_MODEL // 2
N_HEADS = ATTN_WIDTH // HEAD_DIM
GCONV_WIDTH = D_MODEL // 4
CCONV_WIDTH = D_MODEL // 4
GCONV_K = 3
CCONV_K = 31
QKV_WIDTH = 3 * ATTN_WIDTH
REST_WIDTH = 3 * GCONV_WIDTH + 2 * CCONV_WIDTH
D_IN = QKV_WIDTH + REST_WIDTH
DILATIONS = (1, 4, 16)
N_STEPS = 128
BLK = 128
N_CLASSES = max(DILATIONS)
LANES = 128
SUBLANES = 8
CLS_PITCH = BLK + SUBLANES
RMS_EPS = 1e-6
LN_EPS = 1e-5
LOG2E = math.log2(math.e)
ATTN_UNROLL = 4
COMBINE_CHUNK = 256
CONV_ROWS = 64
CONV_SHIFTS = (CCONV_K - 1) // N_CLASSES + 2
VMEM_LIMIT = 56 * 1024 * 1024

F32 = jnp.float32
BF16 = jnp.bfloat16


def _rms(x, g):
    return x * lax.rsqrt(jnp.mean(x * x, axis=-1, keepdims=True) + RMS_EPS) * g


def _params(n_axes):
    return pltpu.CompilerParams(
        dimension_semantics=("arbitrary",) * n_axes, vmem_limit_bytes=VMEM_LIMIT)


def _resident(shape):
    return pl.BlockSpec(shape, lambda *_: (0,) * len(shape), pipeline_mode=pl.Buffered(1))


def _to_class_major_kernel(*refs):
    x_refs, o_ref = refs[:-1], refs[-1]
    cls_rows = o_ref.shape[0] // N_CLASSES

    def one_class(r, carry):
        rows = pl.ds(pl.multiple_of(r * cls_rows, cls_rows), cls_rows)
        for c, x_ref in enumerate(x_refs):
            o_ref[rows, c * LANES:(c + 1) * LANES] = (
                x_ref[pl.ds(r, cls_rows, stride=N_CLASSES), :])
        return carry

    lax.fori_loop(0, N_CLASSES, one_class, 0)


def _to_class_major(x):
    batch, seq, d = x.shape
    slabs = d // LANES
    return pl.pallas_call(
        _to_class_major_kernel, grid=(batch,),
        in_specs=[pl.BlockSpec((None, seq, LANES), lambda b, c=c: (b, 0, c))
                  for c in range(slabs)],
        out_specs=pl.BlockSpec((seq, d), lambda b: (b, 0)),
        out_shape=jax.ShapeDtypeStruct((batch * seq, d), x.dtype),
        compiler_params=_params(1), name="to_class_major")(*([x] * slabs))


def _to_natural_kernel(*refs):
    x_refs, o_ref, stage_ref = refs[:-2], refs[-2], refs[-1]
    cls_rows = o_ref.shape[0] // N_CLASSES
    for c, x_ref in enumerate(x_refs):
        for r in range(N_CLASSES):
            stage_ref[c, r * CLS_PITCH:r * CLS_PITCH + cls_rows, :] = (
                x_ref[r * cls_rows:(r + 1) * cls_rows, :])

        def some_tokens(i, carry, c=c):
            for j in range(SUBLANES):
                l = i * SUBLANES + j
                rows = pl.ds(pl.multiple_of(l * N_CLASSES, N_CLASSES), N_CLASSES)
                o_ref[rows, c * LANES:(c + 1) * LANES] = (
                    stage_ref[c, pl.ds(l, N_CLASSES, stride=CLS_PITCH), :])
            return carry

        lax.fori_loop(0, cls_rows // SUBLANES, some_tokens, 0)


def _to_natural(x, batch, seq):
    d = x.shape[1]
    slabs = d // LANES
    return pl.pallas_call(
        _to_natural_kernel, grid=(batch,),
        in_specs=[pl.BlockSpec((seq, LANES), lambda b, c=c: (b, c)) for c in range(slabs)],
        out_specs=pl.BlockSpec((None, seq, d), lambda b: (b, 0, 0)),
        out_shape=jax.ShapeDtypeStruct((batch, seq, d), x.dtype),
        scratch_shapes=[pltpu.VMEM((slabs, N_CLASSES * CLS_PITCH, LANES), x.dtype)],
        compiler_params=_params(1), name="to_natural")(*([x] * slabs))


def _ffn_kernel(x_ref, g_ref, wg_ref, wu_ref, wd_ref, *rest, final):
    o_ref = rest[-1]
    x = x_ref[...]
    h = _rms(x, g_ref[...]).astype(BF16)
    gate = jnp.dot(h, wg_ref[...], preferred_element_type=F32)
    up = jnp.dot(h, wu_ref[...], preferred_element_type=F32)
    act = (gate * jax.nn.sigmoid(gate) * up).astype(BF16)
    y = x + 0.5 * jnp.dot(act, wd_ref[...], preferred_element_type=F32)
    if final:
        y = _rms(y, rest[0][...])
    o_ref[...] = y


def _ffn(x, gain, wg, wu, wd, final_gain=None, tm=512):
    t = x.shape[0]
    row = pl.BlockSpec((tm, D_MODEL), lambda i: (i, 0))
    in_specs = [row, _resident((1, D_MODEL)), _resident((D_MODEL, D_FF)),
                _resident((D_MODEL, D_FF)), _resident((D_FF, D_MODEL))]
    args = [x, gain.reshape(1, D_MODEL), wg, wu, wd]
    if final_gain is not None:
        in_specs.append(_resident((1, D_MODEL)))
        args.append(final_gain.reshape(1, D_MODEL))
    return pl.pallas_call(
        functools.partial(_ffn_kernel, final=final_gain is not None),
        grid=(t // tm,), in_specs=in_specs, out_specs=row,
        out_shape=jax.ShapeDtypeStruct((t, D_MODEL), F32),
        compiler_params=_params(1), name="ffn")(*args)


def _inproj_kernel(x_ref, g_ref, w_ref, qkv_ref, rest_ref):
    h = _rms(x_ref[...], g_ref[...]).astype(BF16)
    qkv_ref[...] = jnp.dot(h, w_ref[:, :QKV_WIDTH], preferred_element_type=F32)
    rest_ref[...] = jnp.dot(h, w_ref[:, QKV_WIDTH:], preferred_element_type=F32)


def _inproj(x, gain, w_in, tm=512):
    t = x.shape[0]
    return pl.pallas_call(
        _inproj_kernel, grid=(t // tm,),
        in_specs=[pl.BlockSpec((tm, D_MODEL), lambda i: (i, 0)),
                  _resident((1, D_MODEL)), _resident((D_MODEL, D_IN))],
        out_specs=[pl.BlockSpec((tm, QKV_WIDTH), lambda i: (i, 0)),
                   pl.BlockSpec((tm, REST_WIDTH), lambda i: (i, 0))],
        out_shape=[jax.ShapeDtypeStruct((t, QKV_WIDTH), F32),
                   jax.ShapeDtypeStruct((t, REST_WIDTH), F32)],
        compiler_params=_params(1), name="inproj")(x, gain.reshape(1, D_MODEL), w_in)


def _alibi_bias_tables():
    slopes = np.exp2(-8.0 * np.arange(1, N_HEADS + 1, dtype=np.float64) / N_HEADS)
    with_prev, first = [], []
    for dil in DILATIONS:
        groups = N_CLASSES // dil
        piece = BLK // groups
        q_pos = (groups * np.arange(piece)[None, :] + np.arange(groups)[:, None]).reshape(-1)
        for table, key_piece, shift in ((with_prev, 2 * piece, BLK), (first, piece, 0)):
            k_pos = (groups * np.arange(key_piece)[None, :]
                     + np.arange(groups)[:, None]).reshape(-1) - shift
            steps = q_pos[:, None] - k_pos[None, :]
            valid = (steps >= 0) & (steps <= N_STEPS)
            bias = -slopes[:, None, None] * (steps * dil)[None] * LOG2E
            table.append(np.where(valid[None], bias, -np.inf).astype(np.float32))
    return np.stack(with_prev), np.stack(first)


def _attn_kernel(q_ref, k_ref, v_ref, bias_prev_ref, bias_first_ref, o_ref,
                 acc_ref, max_ref, s_ref, p_ref):
    lane = lax.broadcasted_iota(jnp.int32, (1, LANES), 1)
    head_mask = (lane < HEAD_DIM, lane >= HEAD_DIM)

    def pieces(dil, cls, n, with_prev):
        groups = N_CLASSES // dil
        piece = BLK // groups
        first = cls * BLK + (n - 1 if with_prev else n) * piece
        rows = 2 * piece if with_prev else piece
        if not isinstance(first, int):
            first = pl.multiple_of(first, SUBLANES)
        return [((dil * g) * BLK + first, rows) for g in range(groups)]

    def gather(ref, where):
        parts = [ref[pl.ds(start, rows), :] for start, rows in where]
        return parts[0] if len(parts) == 1 else jnp.concatenate(parts, axis=0)

    def scatter(ref, slab, where, value):
        offset = 0
        for start, rows in where:
            ref[slab, pl.ds(start, rows), :] = value[offset:offset + rows]
            offset += rows

    def scores(slot, bi, dil, blocks):
        for u, (cls, n, has_prev) in enumerate(blocks):
            keys = 2 * BLK if has_prev else BLK
            q = gather(q_ref, pieces(dil, cls, n, False)) * (HEAD_DIM ** -0.5 * LOG2E)
            kb = gather(k_ref, pieces(dil, cls, n, has_prev)).astype(BF16)
            for h in range(2):
                qh = jnp.where(head_mask[h], q, 0.0).astype(BF16)
                s = lax.dot_general(qh, kb, (((1,), (1,)), ((), ())),
                                    preferred_element_type=F32)
                bias = bias_prev_ref[bi, h] if has_prev else bias_first_ref[bi, h]
                s_ref[slot, 2 * u + h, :, :keys] = s + bias

    def softmax(slot, bi, dil, blocks):
        for u, (cls, n, has_prev) in enumerate(blocks):
            keys = 2 * BLK if has_prev else BLK
            row_max = []
            for h in range(2):
                s = s_ref[slot, 2 * u + h, :, :keys]
                m = jnp.max(s, axis=1, keepdims=True)
                p_ref[slot, 2 * u + h, :, :keys] = jnp.exp2(s - m).astype(BF16)
                row_max.append(m)
            scatter(max_ref, bi, pieces(dil, cls, n, False),
                    jnp.where(head_mask[0], row_max[0], row_max[1]))

    def weighted_values(slot, bi, dil, blocks):
        for u, (cls, n, has_prev) in enumerate(blocks):
            keys = 2 * BLK if has_prev else BLK
            v = gather(v_ref, pieces(dil, cls, n, has_prev))
            for h in range(2):
                v_aug = jnp.where(head_mask[h], v, 1.0).astype(BF16)
                scatter(acc_ref, 2 * bi + h, pieces(dil, cls, n, False),
                        jnp.dot(p_ref[slot, 2 * u + h, :, :keys], v_aug,
                                preferred_element_type=F32))

    groups = []
    for bi, dil in enumerate(DILATIONS):
        blocks = [(cls, n, n > 0) for cls in range(dil) for n in range(N_CLASSES // dil)]
        groups += [(bi, dil, blocks[i:i + ATTN_UNROLL])
                   for i in range(0, len(blocks), ATTN_UNROLL)]
    for i in range(len(groups) + 2):
        if i < len(groups):
            scores(i % 2, *groups[i])
        if 1 <= i <= len(groups):
            softmax((i - 1) % 2, *groups[i - 1])
        if i >= 2:
            weighted_values(i % 2, *groups[i - 2])

    for c in range(o_ref.shape[0] // COMBINE_CHUNK):
        rows = pl.ds(c * COMBINE_CHUNK, COMBINE_CHUNK)
        ms = [max_ref[bi, rows, :] for bi in range(len(DILATIONS))]
        m = jnp.maximum(jnp.maximum(ms[0], ms[1]), ms[2])
        tot = [0.0, 0.0]
        for bi in range(len(DILATIONS)):
            w = jnp.exp2(ms[bi] - m)
            w_other = pltpu.roll(w, HEAD_DIM, axis=1)
            tot[0] += jnp.where(head_mask[0], w, w_other) * acc_ref[2 * bi, rows, :]
            tot[1] += jnp.where(head_mask[0], w_other, w) * acc_ref[2 * bi + 1, rows, :]
        num = jnp.where(head_mask[0], tot[0], tot[1])
        den = pltpu.roll(jnp.where(head_mask[0], tot[1], tot[0]), HEAD_DIM, axis=1)
        o_ref[rows, :] = (num / den).astype(o_ref.dtype)


def _attention(qkv, bias_prev, bias_first, batch, seq):
    n_pairs = N_HEADS // 2
    n_slabs = 2 * len(DILATIONS)
    n_dil = len(DILATIONS)
    return pl.pallas_call(
        _attn_kernel,
        grid=(batch, n_pairs),
        in_specs=[pl.BlockSpec((seq, LANES), lambda b, p: (b, p)),
                  pl.BlockSpec((seq, LANES), lambda b, p: (b, n_pairs + p)),
                  pl.BlockSpec((seq, LANES), lambda b, p: (b, 2 * n_pairs + p)),
                  pl.BlockSpec((n_dil, 2, BLK, 2 * BLK), lambda b, p: (0, p, 0, 0)),
                  pl.BlockSpec((n_dil, 2, BLK, BLK), lambda b, p: (0, p, 0, 0))],
        out_specs=pl.BlockSpec((seq, LANES), lambda b, p: (b, p)),
        out_shape=jax.ShapeDtypeStruct((batch * seq, ATTN_WIDTH), BF16),
        scratch_shapes=[pltpu.VMEM((n_slabs, seq, LANES), F32),
                        pltpu.VMEM((n_dil, seq, LANES), F32),
                        pltpu.VMEM((2, 2 * ATTN_UNROLL, BLK, 2 * BLK), F32),
                        pltpu.VMEM((2, 2 * ATTN_UNROLL, BLK, 2 * BLK), BF16)],
        compiler_params=_params(2), name="dilated_attn")(qkv, qkv, qkv, bias_prev, bias_first)


def _conv_kernel(z_ref, gw_ref, cw_ref, cb_ref, lg_ref, lb_ref, o_ref, u_ref, pre_ref):
    cls_rows = z_ref.shape[0] // N_CLASSES
    g0, g1, g2 = 0, GCONV_WIDTH, 2 * GCONV_WIDTH
    c0 = 3 * GCONV_WIDTH
    c1 = c0 + CCONV_WIDTH
    row = lax.broadcasted_iota(jnp.int32, (cls_rows, 1), 0)

    def fill(conv_input, n_shifts):
        def body(r, carry):
            rows = pl.ds(pl.multiple_of(r * cls_rows, cls_rows), cls_rows)
            u = conv_input(rows)
            u_ref[0, rows, :] = u
            for d in range(1, n_shifts):
                u_ref[d, rows, :] = jnp.where(row >= d, pltpu.roll(u, d, axis=0), 0.0)
            return carry
        lax.fori_loop(0, N_CLASSES, body, 0)

    def conv_tile(w_ref, n_taps, r, row0):
        acc = None
        for s in range(n_taps):
            whole, part = divmod(s, N_CLASSES)
            borrow = jnp.where(r < part, 1, 0) if part else 0
            src_cls = r - part + N_CLASSES * borrow
            start = pl.multiple_of(src_cls * cls_rows + row0, CONV_ROWS)
            src = u_ref[whole + borrow, pl.ds(start, CONV_ROWS), :]
            term = w_ref[n_taps - 1 - s:n_taps - s, :] * src
            acc = term if acc is None else acc + term
        return acc

    def tiles(body):
        def step(i, carry):
            r = i // (cls_rows // CONV_ROWS)
            row0 = (i % (cls_rows // CONV_ROWS)) * CONV_ROWS
            body(r, row0, pl.ds(pl.multiple_of(r * cls_rows + row0, CONV_ROWS), CONV_ROWS))
            return carry
        lax.fori_loop(0, N_CLASSES * (cls_rows // CONV_ROWS), step, 0)

    fill(lambda rows: z_ref[rows, g1:g2] * z_ref[rows, g2:c0],
         (GCONV_K - 1) // N_CLASSES + 2)

    def gconv(r, row0, rows):
        acc = conv_tile(gw_ref, GCONV_K, r, row0)
        o_ref[rows, 0:GCONV_WIDTH] = (z_ref[rows, g0:g1] * acc).astype(o_ref.dtype)

    tiles(gconv)

    fill(lambda rows: z_ref[rows, c0:c1] * jax.nn.sigmoid(z_ref[rows, c1:]), CONV_SHIFTS)

    def conv_class(r):
        for row0 in range(0, cls_rows, CONV_ROWS):
            rows = pl.ds(pl.multiple_of(r * cls_rows + row0, CONV_ROWS), CONV_ROWS)
            pre_ref[rows, :] = conv_tile(cw_ref, CCONV_K, r, row0) + cb_ref[...]

    def norm_class(r):
        rows = pl.ds(pl.multiple_of(r * cls_rows, cls_rows), cls_rows)
        u = pre_ref[rows, :]
        mu = jnp.mean(u, axis=-1, keepdims=True)
        d = u - mu
        var = jnp.mean(d * d, axis=-1, keepdims=True)
        y = d * lax.rsqrt(var + LN_EPS) * lg_ref[...] + lb_ref[...]
        o_ref[rows, GCONV_WIDTH:] = (y * jax.nn.sigmoid(y)).astype(o_ref.dtype)

    conv_class(0)

    def step(r, carry):
        norm_class(r - 1)
        conv_class(r)
        return carry

    lax.fori_loop(1, N_CLASSES, step, 0)
    norm_class(N_CLASSES - 1)


def _conv_mixers(rest, gconv_w, cconv_w, cconv_b, cln_g, cln_b, batch, seq):
    width = GCONV_WIDTH + CCONV_WIDTH
    return pl.pallas_call(
        _conv_kernel,
        grid=(batch,),
        in_specs=[pl.BlockSpec((seq, REST_WIDTH), lambda b: (b, 0)),
                  _resident((GCONV_K, GCONV_WIDTH)), _resident((CCONV_K, CCONV_WIDTH)),
                  _resident((1, CCONV_WIDTH)), _resident((1, CCONV_WIDTH)),
                  _resident((1, CCONV_WIDTH))],
        out_specs=pl.BlockSpec((seq, width), lambda b: (b, 0)),
        out_shape=jax.ShapeDtypeStruct((batch * seq, width), BF16),
        scratch_shapes=[pltpu.VMEM((CONV_SHIFTS, seq, GCONV_WIDTH), F32),
                        pltpu.VMEM((seq, CCONV_WIDTH), F32)],
        compiler_params=_params(1), name="conv_mixers")(
            rest, gconv_w, cconv_w, cconv_b.reshape(1, -1), cln_g.reshape(1, -1),
            cln_b.reshape(1, -1))


def _outproj_kernel(x_ref, ya_ref, yc_ref, w_ref, o_ref):
    o_ref[...] = (x_ref[...]
                  + jnp.dot(ya_ref[...], w_ref[:ATTN_WIDTH, :], preferred_element_type=F32)
                  + jnp.dot(yc_ref[...], w_ref[ATTN_WIDTH:, :], preferred_element_type=F32))


def _outproj(x, y_attn, y_conv, w_out, tm=1024):
    t = x.shape[0]
    row = pl.BlockSpec((tm, D_MODEL), lambda i: (i, 0))
    half = pl.BlockSpec((tm, ATTN_WIDTH), lambda i: (i, 0))
    return pl.pallas_call(
        _outproj_kernel, grid=(t // tm,),
        in_specs=[row, half, half, _resident((D_MODEL, D_MODEL))],
        out_specs=row, out_shape=jax.ShapeDtypeStruct((t, D_MODEL), F32),
        compiler_params=_params(1), name="outproj")(x, y_attn, y_conv, w_out)


def kernel(x, w_in, w_out, gconv_w, cconv_w, cconv_b, cln_g, cln_b, ffn1_wg, ffn1_wu, ffn1_wd, ffn2_wg, ffn2_wu, ffn2_wd, norm_ffn1, norm_mix, norm_ffn2, norm_final):
    batch, seq, _ = x.shape
    depth = w_in.shape[0]
    assert seq == N_CLASSES * BLK and x.shape[2] == D_MODEL
    bias_prev, bias_first = (jnp.asarray(t) for t in _alibi_bias_tables())
    h = _to_class_major(x)
    for l in range(depth):
        h = _ffn(h, norm_ffn1[l], ffn1_wg[l].astype(BF16), ffn1_wu[l].astype(BF16),
                 ffn1_wd[l].astype(BF16))
        qkv, rest = _inproj(h, norm_mix[l], w_in[l].astype(BF16))
        y_attn = _attention(qkv, bias_prev, bias_first, batch, seq)
        y_conv = _conv_mixers(rest, gconv_w[l], cconv_w[l], cconv_b[l], cln_g[l],
                              cln_b[l], batch, seq)
        h = _outproj(h, y_attn, y_conv, w_out[l].astype(BF16))
        h = _ffn(h, norm_ffn2[l], ffn2_wg[l].astype(BF16), ffn2_wu[l].astype(BF16),
                 ffn2_wd[l].astype(BF16),
                 final_gain=norm_final if l == depth - 1 else None)
    return _to_natural(h, batch, seq)
```

```python
import functools
import math

import numpy as np
import jax
import jax.numpy as jnp
from jax import lax
from jax.experimental import pallas as pl
from jax.experimental.pallas import tpu as pltpu

D_MODEL = 1024
D_FF = 11 * D_MODEL // 4
HEAD_DIM = 64
ATTN_WIDTH = D_MODEL // 2
N_HEADS = ATTN_WIDTH // HEAD_DIM
GCONV_WIDTH = D_MODEL // 4
CCONV_WIDTH = D_MODEL // 4
GCONV_K = 3
CCONV_K = 31
QKV_WIDTH = 3 * ATTN_WIDTH
REST_WIDTH = 3 * GCONV_WIDTH + 2 * CCONV_WIDTH
D_IN = QKV_WIDTH + REST_WIDTH
DILATIONS = (1, 4, 16)
N_STEPS = 128
BLK = 128
N_CLASSES = max(DILATIONS)
LANES = 128
SUBLANES = 8
NATURAL_CLASSES = SUBLANES
FFN_ROWS = 512
RMS_EPS = 1e-6
LN_EPS = 1e-5
LOG2E = math.log2(math.e)
ATTN_UNROLL = 4
COMBINE_CHUNK = 256
CONV_ROWS = 64
CONV_SHIFTS = (CCONV_K - 1) // N_CLASSES + 2
VMEM_LIMIT = 56 * 1024 * 1024

F32 = jnp.float32
BF16 = jnp.bfloat16


def _rms(x, g):
    return x * lax.rsqrt(jnp.mean(x * x, axis=-1, keepdims=True) + RMS_EPS) * g


def _params(n_axes):
    return pltpu.CompilerParams(
        dimension_semantics=("arbitrary",) * n_axes, vmem_limit_bytes=VMEM_LIMIT)


def _resident(shape):
    return pl.BlockSpec(shape, lambda *_: (0,) * len(shape), pipeline_mode=pl.Buffered(1))


def _ffn_kernel(x_ref, g_ref, wg_ref, wu_ref, wd_ref, *rest, final, natural_in,
                natural_out, project):
    o_ref = rest[-1]
    cls_per_sub = FFN_ROWS // BLK

    def sub_tile(i):
        rows = pl.ds(pl.multiple_of(i * FFN_ROWS, FFN_ROWS), FFN_ROWS)
        if natural_in:
            x = jnp.concatenate(
                [x_ref[:, i * cls_per_sub + c, :] for c in range(cls_per_sub)], axis=0)
        else:
            x = x_ref[rows, :]
        if project:
            ya_ref, yc_ref, wo_ref = rest[:3]
            x = (x + jnp.dot(ya_ref[rows, :], wo_ref[:ATTN_WIDTH, :],
                             preferred_element_type=F32)
                 + jnp.dot(yc_ref[rows, :], wo_ref[ATTN_WIDTH:, :],
                           preferred_element_type=F32))
        h = _rms(x, g_ref[...]).astype(BF16)
        gate = jnp.dot(h, wg_ref[...], preferred_element_type=F32)
        up = jnp.dot(h, wu_ref[...], preferred_element_type=F32)
        act = (gate * jax.nn.sigmoid(gate) * up).astype(BF16)
        y = x + 0.5 * jnp.dot(act, wd_ref[...], preferred_element_type=F32)
        if final:
            y = _rms(y, rest[-2][...])
        if natural_out:
            for c in range(cls_per_sub):
                o_ref[:, i * cls_per_sub + c, :] = y[c * BLK:(c + 1) * BLK]
        else:
            o_ref[rows, :] = y

    n_sub = (x_ref.shape[1] * BLK if natural_in else x_ref.shape[0]) // FFN_ROWS
    if n_sub == 1:
        sub_tile(0)
    else:
        def step(i, carry):
            sub_tile(i)
            return carry
        lax.fori_loop(0, n_sub, step, 0)


def _ffn(x, gain, wg, wu, wd, final_gain=None, natural_in=None, natural_out=None,
         mixed=None):
    natural = natural_in or natural_out
    if natural:
        batch, seq = natural
        assert seq == N_CLASSES * BLK
        t = batch * seq
        halves = N_CLASSES // NATURAL_CLASSES
        nat_block = pl.BlockSpec((None, BLK, NATURAL_CLASSES, D_MODEL),
                                 lambda b, j: (b, 0, j, 0))
        cls_block = pl.BlockSpec((NATURAL_CLASSES * BLK, D_MODEL),
                                 lambda b, j: (b * halves + j, 0))
        grid = (batch, halves)
        x_spec = nat_block if natural_in else cls_block
        out_spec = nat_block if natural_out else cls_block
        y_spec = pl.BlockSpec((NATURAL_CLASSES * BLK, ATTN_WIDTH),
                              lambda b, j: (b * halves + j, 0))
        if natural_in:
            x = x.reshape(batch, BLK, N_CLASSES, D_MODEL)
    else:
        t = x.shape[0]
        grid = (t // FFN_ROWS,)
        x_spec = out_spec = pl.BlockSpec((FFN_ROWS, D_MODEL), lambda i: (i, 0))
        y_spec = pl.BlockSpec((FFN_ROWS, ATTN_WIDTH), lambda i: (i, 0))
    out_shape = ((batch, BLK, N_CLASSES, D_MODEL) if natural_out else (t, D_MODEL))
    in_specs = [x_spec, _resident((1, D_MODEL)), _resident((D_MODEL, D_FF)),
                _resident((D_MODEL, D_FF)), _resident((D_FF, D_MODEL))]
    args = [x, gain.reshape(1, D_MODEL), wg, wu, wd]
    if mixed is not None:
        in_specs += [y_spec, y_spec, _resident((D_MODEL, D_MODEL))]
        args += list(mixed)
    if final_gain is not None:
        in_specs.append(_resident((1, D_MODEL)))
        args.append(final_gain.reshape(1, D_MODEL))
    out = pl.pallas_call(
        functools.partial(_ffn_kernel, final=final_gain is not None,
                          natural_in=bool(natural_in), natural_out=bool(natural_out),
                          project=mixed is not None),
        grid=grid, in_specs=in_specs, out_specs=out_spec,
        out_shape=jax.ShapeDtypeStruct(out_shape, F32),
        compiler_params=_params(len(grid)), name="ffn")(*args)
    return out.reshape(batch, seq, D_MODEL) if natural_out else out


def _inproj_kernel(x_ref, g_ref, w_ref, qkv_ref, rest_ref):
    h = _rms(x_ref[...], g_ref[...]).astype(BF16)
    qkv_ref[...] = jnp.dot(h, w_ref[:, :QKV_WIDTH], preferred_element_type=F32)
    rest_ref[...] = jnp.dot(h, w_ref[:, QKV_WIDTH:], preferred_element_type=F32)


def _inproj(x, gain, w_in, tm=512):
    t = x.shape[0]
    return pl.pallas_call(
        _inproj_kernel, grid=(t // tm,),
        in_specs=[pl.BlockSpec((tm, D_MODEL), lambda i: (i, 0)),
                  _resident((1, D_MODEL)), _resident((D_MODEL, D_IN))],
        out_specs=[pl.BlockSpec((tm, QKV_WIDTH), lambda i: (i, 0)),
                   pl.BlockSpec((tm, REST_WIDTH), lambda i: (i, 0))],
        out_shape=[jax.ShapeDtypeStruct((t, QKV_WIDTH), F32),
                   jax.ShapeDtypeStruct((t, REST_WIDTH), F32)],
        compiler_params=_params(1), name="inproj")(x, gain.reshape(1, D_MODEL), w_in)


def _alibi_bias_tables():
    slopes = np.exp2(-8.0 * np.arange(1, N_HEADS + 1, dtype=np.float64) / N_HEADS)
    with_prev, first = [], []
    for dil in DILATIONS:
        groups = N_CLASSES // dil
        piece = BLK // groups
        q_pos = (groups * np.arange(piece)[None, :] + np.arange(groups)[:, None]).reshape(-1)
        for table, key_piece, shift in ((with_prev, 2 * piece, BLK), (first, piece, 0)):
            k_pos = (groups * np.arange(key_piece)[None, :]
                     + np.arange(groups)[:, None]).reshape(-1) - shift
            steps = q_pos[:, None] - k_pos[None, :]
            valid = (steps >= 0) & (steps <= N_STEPS)
            bias = -slopes[:, None, None] * (steps * dil)[None] * LOG2E
            table.append(np.where(valid[None], bias, -np.inf).astype(np.float32))
    return np.stack(with_prev), np.stack(first)


def _attn_kernel(q_ref, k_ref, v_ref, bias_prev_ref, bias_first_ref, o_ref,
                 acc_ref, max_ref, s_ref, p_ref):
    lane = lax.broadcasted_iota(jnp.int32, (1, LANES), 1)
    head_mask = (lane < HEAD_DIM, lane >= HEAD_DIM)

    def pieces(dil, cls, n, with_prev):
        groups = N_CLASSES // dil
        piece = BLK // groups
        first = cls * BLK + (n - 1 if with_prev else n) * piece
        rows = 2 * piece if with_prev else piece
        if not isinstance(first, int):
            first = pl.multiple_of(first, SUBLANES)
        return [((dil * g) * BLK + first, rows) for g in range(groups)]

    def gather(ref, where):
        parts = [ref[pl.ds(start, rows), :] for start, rows in where]
        return parts[0] if len(parts) == 1 else jnp.concatenate(parts, axis=0)

    def scatter(ref, slab, where, value):
        offset = 0
        for start, rows in where:
            ref[slab, pl.ds(start, rows), :] = value[offset:offset + rows]
            offset += rows

    def scores(slot, bi, dil, blocks):
        for u, (cls, n, has_prev) in enumerate(blocks):
            keys = 2 * BLK if has_prev else BLK
            q = gather(q_ref, pieces(dil, cls, n, False)) * (HEAD_DIM ** -0.5 * LOG2E)
            kb = gather(k_ref, pieces(dil, cls, n, has_prev)).astype(BF16)
            for h in range(2):
                qh = jnp.where(head_mask[h], q, 0.0).astype(BF16)
                s = lax.dot_general(qh, kb, (((1,), (1,)), ((), ())),
                                    preferred_element_type=F32)
                bias = bias_prev_ref[bi, h] if has_prev else bias_first_ref[bi, h]
                s_ref[slot, 2 * u + h, :, :keys] = s + bias

    def softmax(slot, bi, dil, blocks):
        for u, (cls, n, has_prev) in enumerate(blocks):
            keys = 2 * BLK if has_prev else BLK
            row_max = []
            for h in range(2):
                s = s_ref[slot, 2 * u + h, :, :keys]
                m = jnp.max(s, axis=1, keepdims=True)
                p_ref[slot, 2 * u + h, :, :keys] = jnp.exp2(s - m).astype(BF16)
                row_max.append(m)
            scatter(max_ref, bi, pieces(dil, cls, n, False),
                    jnp.where(head_mask[0], row_max[0], row_max[1]))

    def weighted_values(slot, bi, dil, blocks):
        for u, (cls, n, has_prev) in enumerate(blocks):
            keys = 2 * BLK if has_prev else BLK
            v = gather(v_ref, pieces(dil, cls, n, has_prev))
            for h in range(2):
                v_aug = jnp.where(head_mask[h], v, 1.0).astype(BF16)
                scatter(acc_ref, 2 * bi + h, pieces(dil, cls, n, False),
                        jnp.dot(p_ref[slot, 2 * u + h, :, :keys], v_aug,
                                preferred_element_type=F32))

    groups = []
    for bi, dil in enumerate(DILATIONS):
        blocks = [(cls, n, n > 0) for cls in range(dil) for n in range(N_CLASSES // dil)]
        groups += [(bi, dil, blocks[i:i + ATTN_UNROLL])
                   for i in range(0, len(blocks), ATTN_UNROLL)]
    for i in range(len(groups) + 2):
        if i < len(groups):
            scores(i % 2, *groups[i])
        if 1 <= i <= len(groups):
            softmax((i - 1) % 2, *groups[i - 1])
        if i >= 2:
            weighted_values(i % 2, *groups[i - 2])

    for c in range(o_ref.shape[0] // COMBINE_CHUNK):
        rows = pl.ds(c * COMBINE_CHUNK, COMBINE_CHUNK)
        ms = [max_ref[bi, rows, :] for bi in range(len(DILATIONS))]
        m = jnp.maximum(jnp.maximum(ms[0], ms[1]), ms[2])
        tot = [0.0, 0.0]
        for bi in range(len(DILATIONS)):
            w = jnp.exp2(ms[bi] - m)
            w_other = pltpu.roll(w, HEAD_DIM, axis=1)
            tot[0] += jnp.where(head_mask[0], w, w_other) * acc_ref[2 * bi, rows, :]
            tot[1] += jnp.where(head_mask[0], w_other, w) * acc_ref[2 * bi + 1, rows, :]
        num = jnp.where(head_mask[0], tot[0], tot[1])
        den = pltpu.roll(jnp.where(head_mask[0], tot[1], tot[0]), HEAD_DIM, axis=1)
        o_ref[rows, :] = (num / den).astype(o_ref.dtype)


def _attention(qkv, bias_prev, bias_first, batch, seq):
    n_pairs = N_HEADS // 2
    n_slabs = 2 * len(DILATIONS)
    n_dil = len(DILATIONS)
    return pl.pallas_call(
        _attn_kernel,
        grid=(batch, n_pairs),
        in_specs=[pl.BlockSpec((seq, LANES), lambda b, p: (b, p)),
                  pl.BlockSpec((seq, LANES), lambda b, p: (b, n_pairs + p)),
                  pl.BlockSpec((seq, LANES), lambda b, p: (b, 2 * n_pairs + p)),
                  pl.BlockSpec((n_dil, 2, BLK, 2 * BLK), lambda b, p: (0, p, 0, 0)),
                  pl.BlockSpec((n_dil, 2, BLK, BLK), lambda b, p: (0, p, 0, 0))],
        out_specs=pl.BlockSpec((seq, LANES), lambda b, p: (b, p)),
        out_shape=jax.ShapeDtypeStruct((batch * seq, ATTN_WIDTH), BF16),
        scratch_shapes=[pltpu.VMEM((n_slabs, seq, LANES), F32),
                        pltpu.VMEM((n_dil, seq, LANES), F32),
                        pltpu.VMEM((2, 2 * ATTN_UNROLL, BLK, 2 * BLK), F32),
                        pltpu.VMEM((2, 2 * ATTN_UNROLL, BLK, 2 * BLK), BF16)],
        compiler_params=_params(2), name="dilated_attn")(qkv, qkv, qkv, bias_prev, bias_first)


def _conv_kernel(z_ref, gw_ref, cw_ref, cb_ref, lg_ref, lb_ref, o_ref, u_ref, pre_ref):
    cls_rows = z_ref.shape[0] // N_CLASSES
    g0, g1, g2 = 0, GCONV_WIDTH, 2 * GCONV_WIDTH
    c0 = 3 * GCONV_WIDTH
    c1 = c0 + CCONV_WIDTH
    row = lax.broadcasted_iota(jnp.int32, (cls_rows, 1), 0)

    def fill(conv_input, n_shifts):
        def body(r, carry):
            rows = pl.ds(pl.multiple_of(r * cls_rows, cls_rows), cls_rows)
            u = conv_input(rows)
            u_ref[0, rows, :] = u
            for d in range(1, n_shifts):
                u_ref[d, rows, :] = jnp.where(row >= d, pltpu.roll(u, d, axis=0), 0.0)
            return carry
        lax.fori_loop(0, N_CLASSES, body, 0)

    def conv_tile(w_ref, n_taps, r, row0):
        acc = None
        for s in range(n_taps):
            whole, part = divmod(s, N_CLASSES)
            borrow = jnp.where(r < part, 1, 0) if part else 0
            src_cls = r - part + N_CLASSES * borrow
            start = pl.multiple_of(src_cls * cls_rows + row0, CONV_ROWS)
            src = u_ref[whole + borrow, pl.ds(start, CONV_ROWS), :]
            term = w_ref[n_taps - 1 - s:n_taps - s, :] * src
            acc = term if acc is None else acc + term
        return acc

    def tiles(body):
        def step(i, carry):
            r = i // (cls_rows // CONV_ROWS)
            row0 = (i % (cls_rows // CONV_ROWS)) * CONV_ROWS
            body(r, row0, pl.ds(pl.multiple_of(r * cls_rows + row0, CONV_ROWS), CONV_ROWS))
            return carry
        lax.fori_loop(0, N_CLASSES * (cls_rows // CONV_ROWS), step, 0)

    fill(lambda rows: z_ref[rows, g1:g2] * z_ref[rows, g2:c0],
         (GCONV_K - 1) // N_CLASSES + 2)

    def gconv(r, row0, rows):
        acc = conv_tile(gw_ref, GCONV_K, r, row0)
        o_ref[rows, 0:GCONV_WIDTH] = (z_ref[rows, g0:g1] * acc).astype(o_ref.dtype)

    tiles(gconv)

    fill(lambda rows: z_ref[rows, c0:c1] * jax.nn.sigmoid(z_ref[rows, c1:]), CONV_SHIFTS)

    def conv_class(r):
        for row0 in range(0, cls_rows, CONV_ROWS):
            rows = pl.ds(pl.multiple_of(r * cls_rows + row0, CONV_ROWS), CONV_ROWS)
            pre_ref[rows, :] = conv_tile(cw_ref, CCONV_K, r, row0) + cb_ref[...]

    def norm_class(r):
        rows = pl.ds(pl.multiple_of(r * cls_rows, cls_rows), cls_rows)
        u = pre_ref[rows, :]
        mu = jnp.mean(u, axis=-1, keepdims=True)
        d = u - mu
        var = jnp.mean(d * d, axis=-1, keepdims=True)
        y = d * lax.rsqrt(var + LN_EPS) * lg_ref[...] + lb_ref[...]
        o_ref[rows, GCONV_WIDTH:] = (y * jax.nn.sigmoid(y)).astype(o_ref.dtype)

    conv_class(0)

    def step(r, carry):
        norm_class(r - 1)
        conv_class(r)
        return carry

    lax.fori_loop(1, N_CLASSES, step, 0)
    norm_class(N_CLASSES - 1)


def _conv_mixers(rest, gconv_w, cconv_w, cconv_b, cln_g, cln_b, batch, seq):
    width = GCONV_WIDTH + CCONV_WIDTH
    return pl.pallas_call(
        _conv_kernel,
        grid=(batch,),
        in_specs=[pl.BlockSpec((seq, REST_WIDTH), lambda b: (b, 0)),
                  _resident((GCONV_K, GCONV_WIDTH)), _resident((CCONV_K, CCONV_WIDTH)),
                  _resident((1, CCONV_WIDTH)), _resident((1, CCONV_WIDTH)),
                  _resident((1, CCONV_WIDTH))],
        out_specs=pl.BlockSpec((seq, width), lambda b: (b, 0)),
        out_shape=jax.ShapeDtypeStruct((batch * seq, width), BF16),
        scratch_shapes=[pltpu.VMEM((CONV_SHIFTS, seq, GCONV_WIDTH), F32),
                        pltpu.VMEM((seq, CCONV_WIDTH), F32)],
        compiler_params=_params(1), name="conv_mixers")(
            rest, gconv_w, cconv_w, cconv_b.reshape(1, -1), cln_g.reshape(1, -1),
            cln_b.reshape(1, -1))


def kernel(x, w_in, w_out, gconv_w, cconv_w, cconv_b, cln_g, cln_b, ffn1_wg, ffn1_wu, ffn1_wd, ffn2_wg, ffn2_wu, ffn2_wd, norm_ffn1, norm_mix, norm_ffn2, norm_final):
    batch, seq, _ = x.shape
    depth = w_in.shape[0]
    assert seq == N_CLASSES * BLK and x.shape[2] == D_MODEL
    bias_prev, bias_first = (jnp.asarray(t) for t in _alibi_bias_tables())
    h = x
    for l in range(depth):
        last = l == depth - 1
        h = _ffn(h, norm_ffn1[l], ffn1_wg[l].astype(BF16), ffn1_wu[l].astype(BF16),
                 ffn1_wd[l].astype(BF16), natural_in=(batch, seq) if l == 0 else None)
        qkv, rest = _inproj(h, norm_mix[l], w_in[l].astype(BF16))
        y_attn = _attention(qkv, bias_prev, bias_first, batch, seq)
        y_conv = _conv_mixers(rest, gconv_w[l], cconv_w[l], cconv_b[l], cln_g[l],
                              cln_b[l], batch, seq)
        h = _ffn(h, norm_ffn2[l], ffn2_wg[l].astype(BF16), ffn2_wu[l].astype(BF16),
                 ffn2_wd[l].astype(BF16), final_gain=norm_final if last else None,
                 natural_out=(batch, seq) if last else None,
                 mixed=(y_attn, y_conv, w_out[l].astype(BF16)))
    return h
```

```python
import functools
import math

import numpy as np
import jax
import jax.numpy as jnp
from jax import lax
from jax.experimental import pallas as pl
from jax.experimental.pallas import tpu as pltpu

D_MODEL = 1024
D_FF = 11 * D_MODEL // 4
HEAD_DIM = 64
ATTN_WIDTH = D_MODEL // 2
N_HEADS = ATTN_WIDTH // HEAD_DIM
GCONV_WIDTH = D_MODEL // 4
CCONV_WIDTH = D_MODEL // 4
GCONV_K = 3
CCONV_K = 31
QKV_WIDTH = 3 * ATTN_WIDTH
REST_WIDTH = 3 * GCONV_WIDTH + 2 * CCONV_WIDTH
D_IN = QKV_WIDTH + REST_WIDTH
DILATIONS = (1, 4, 16)
N_STEPS = 128
BLK = 128
N_CLASSES = max(DILATIONS)
LANES = 128
SUBLANES = 8
NATURAL_CLASSES = SUBLANES
FFN_ROWS = 512
RMS_EPS = 1e-6
LN_EPS = 1e-5
LOG2E = math.log2(math.e)
ATTN_UNROLL = 4
COMBINE_CHUNK = 256
CONV_ROWS = 64
CONV_SHIFTS = (CCONV_K - 1) // N_CLASSES + 2
VMEM_LIMIT = 56 * 1024 * 1024

F32 = jnp.float32
BF16 = jnp.bfloat16


def _rms(x, g):
    return x * lax.rsqrt(jnp.mean(x * x, axis=-1, keepdims=True) + RMS_EPS) * g


def _params(n_axes):
    return pltpu.CompilerParams(
        dimension_semantics=("arbitrary",) * n_axes, vmem_limit_bytes=VMEM_LIMIT)


def _resident(shape, layer=None):
    if layer is None:
        return pl.BlockSpec(shape, lambda *_: (0,) * len(shape),
                            pipeline_mode=pl.Buffered(1))
    return pl.BlockSpec((None,) + shape, lambda *_: (layer,) + (0,) * len(shape),
                        pipeline_mode=pl.Buffered(1))


def _ffn_kernel(x_ref, g_ref, wg_ref, wu_ref, wd_ref, *rest, final, natural_in,
                natural_out, project):
    o_ref = rest[-1]
    cls_per_sub = FFN_ROWS // BLK

    def sub_tile(i):
        rows = pl.ds(pl.multiple_of(i * FFN_ROWS, FFN_ROWS), FFN_ROWS)
        if natural_in:
            x = jnp.concatenate(
                [x_ref[:, i * cls_per_sub + c, :] for c in range(cls_per_sub)], axis=0)
        else:
            x = x_ref[rows, :]
        if project:
            ya_ref, yc_ref, wo_ref = rest[:3]
            x = (x + jnp.dot(ya_ref[rows, :], wo_ref[:ATTN_WIDTH, :],
                             preferred_element_type=F32)
                 + jnp.dot(yc_ref[rows, :], wo_ref[ATTN_WIDTH:, :],
                           preferred_element_type=F32))
        h = _rms(x, g_ref[...]).astype(BF16)
        gate = jnp.dot(h, wg_ref[...], preferred_element_type=F32)
        up = jnp.dot(h, wu_ref[...], preferred_element_type=F32)
        act = (gate * jax.nn.sigmoid(gate) * up).astype(BF16)
        y = x + 0.5 * jnp.dot(act, wd_ref[...], preferred_element_type=F32)
        if final:
            y = _rms(y, rest[-2][...])
        if natural_out:
            for c in range(cls_per_sub):
                o_ref[:, i * cls_per_sub + c, :] = y[c * BLK:(c + 1) * BLK]
        else:
            o_ref[rows, :] = y

    n_sub = (x_ref.shape[1] * BLK if natural_in else x_ref.shape[0]) // FFN_ROWS
    if n_sub == 1:
        sub_tile(0)
    else:
        def step(i, carry):
            sub_tile(i)
            return carry
        lax.fori_loop(0, n_sub, step, 0)


def _ffn(x, layer, gain, wg, wu, wd, final_gain=None, natural_in=None, natural_out=None,
         mixed=None):
    natural = natural_in or natural_out
    if natural:
        batch, seq = natural
        assert seq == N_CLASSES * BLK
        t = batch * seq
        halves = N_CLASSES // NATURAL_CLASSES
        nat_block = pl.BlockSpec((None, BLK, NATURAL_CLASSES, D_MODEL),
                                 lambda b, j: (b, 0, j, 0))
        cls_block = pl.BlockSpec((NATURAL_CLASSES * BLK, D_MODEL),
                                 lambda b, j: (b * halves + j, 0))
        grid = (batch, halves)
        x_spec = nat_block if natural_in else cls_block
        out_spec = nat_block if natural_out else cls_block
        y_spec = pl.BlockSpec((NATURAL_CLASSES * BLK, ATTN_WIDTH),
                              lambda b, j: (b * halves + j, 0))
        if natural_in:
            x = x.reshape(batch, BLK, N_CLASSES, D_MODEL)
    else:
        t = x.shape[0]
        grid = (t // FFN_ROWS,)
        x_spec = out_spec = pl.BlockSpec((FFN_ROWS, D_MODEL), lambda i: (i, 0))
        y_spec = pl.BlockSpec((FFN_ROWS, ATTN_WIDTH), lambda i: (i, 0))
    out_shape = ((batch, BLK, N_CLASSES, D_MODEL) if natural_out else (t, D_MODEL))
    in_specs = [x_spec, _resident((1, D_MODEL)), _resident((D_MODEL, D_FF), layer),
                _resident((D_MODEL, D_FF), layer), _resident((D_FF, D_MODEL), layer)]
    args = [x, gain.reshape(1, D_MODEL), wg, wu, wd]
    if mixed is not None:
        in_specs += [y_spec, y_spec, _resident((D_MODEL, D_MODEL), layer)]
        args += list(mixed)
    if final_gain is not None:
        in_specs.append(_resident((1, D_MODEL)))
        args.append(final_gain.reshape(1, D_MODEL))
    out = pl.pallas_call(
        functools.partial(_ffn_kernel, final=final_gain is not None,
                          natural_in=bool(natural_in), natural_out=bool(natural_out),
                          project=mixed is not None),
        grid=grid, in_specs=in_specs, out_specs=out_spec,
        out_shape=jax.ShapeDtypeStruct(out_shape, F32),
        compiler_params=_params(len(grid)), name="ffn")(*args)
    return out.reshape(batch, seq, D_MODEL) if natural_out else out


def _inproj_kernel(x_ref, g_ref, w_ref, qkv_ref, rest_ref):
    h = _rms(x_ref[...], g_ref[...]).astype(BF16)
    qkv_ref[...] = jnp.dot(h, w_ref[:, :QKV_WIDTH], preferred_element_type=F32)
    rest_ref[...] = jnp.dot(h, w_ref[:, QKV_WIDTH:], preferred_element_type=F32)


def _inproj(x, layer, gain, w_in, tm=1024):
    t = x.shape[0]
    return pl.pallas_call(
        _inproj_kernel, grid=(t // tm,),
        in_specs=[pl.BlockSpec((tm, D_MODEL), lambda i: (i, 0)),
                  _resident((1, D_MODEL)), _resident((D_MODEL, D_IN), layer)],
        out_specs=[pl.BlockSpec((tm, QKV_WIDTH), lambda i: (i, 0)),
                   pl.BlockSpec((tm, REST_WIDTH), lambda i: (i, 0))],
        out_shape=[jax.ShapeDtypeStruct((t, QKV_WIDTH), F32),
                   jax.ShapeDtypeStruct((t, REST_WIDTH), F32)],
        compiler_params=_params(1), name="inproj")(x, gain.reshape(1, D_MODEL), w_in)


def _alibi_bias_tables():
    slopes = np.exp2(-8.0 * np.arange(1, N_HEADS + 1, dtype=np.float64) / N_HEADS)
    with_prev, first = [], []
    for dil in DILATIONS:
        groups = N_CLASSES // dil
        piece = BLK // groups
        q_pos = (groups * np.arange(piece)[None, :] + np.arange(groups)[:, None]).reshape(-1)
        for table, key_piece, shift in ((with_prev, 2 * piece, BLK), (first, piece, 0)):
            k_pos = (groups * np.arange(key_piece)[None, :]
                     + np.arange(groups)[:, None]).reshape(-1) - shift
            steps = q_pos[:, None] - k_pos[None, :]
            valid = (steps >= 0) & (steps <= N_STEPS)
            bias = -slopes[:, None, None] * (steps * dil)[None] * LOG2E
            table.append(np.where(valid[None], bias, -np.inf).astype(np.float32))
    return np.stack(with_prev), np.stack(first)


def _attn_kernel(q_ref, k_ref, v_ref, bias_prev_ref, bias_first_ref, o_ref,
                 acc_ref, max_ref, s_ref, p_ref):
    lane = lax.broadcasted_iota(jnp.int32, (1, LANES), 1)
    head_mask = (lane < HEAD_DIM, lane >= HEAD_DIM)

    def pieces(dil, cls, n, with_prev):
        groups = N_CLASSES // dil
        piece = BLK // groups
        first = cls * BLK + (n - 1 if with_prev else n) * piece
        rows = 2 * piece if with_prev else piece
        if not isinstance(first, int):
            first = pl.multiple_of(first, SUBLANES)
        return [((dil * g) * BLK + first, rows) for g in range(groups)]

    def gather(ref, where):
        parts = [ref[pl.ds(start, rows), :] for start, rows in where]
        return parts[0] if len(parts) == 1 else jnp.concatenate(parts, axis=0)

    def scatter(ref, slab, where, value):
        offset = 0
        for start, rows in where:
            ref[slab, pl.ds(start, rows), :] = value[offset:offset + rows]
            offset += rows

    def scores(slot, bi, dil, blocks):
        for u, (cls, n, has_prev) in enumerate(blocks):
            keys = 2 * BLK if has_prev else BLK
            q = gather(q_ref, pieces(dil, cls, n, False)) * (HEAD_DIM ** -0.5 * LOG2E)
            kb = gather(k_ref, pieces(dil, cls, n, has_prev)).astype(BF16)
            for h in range(2):
                qh = jnp.where(head_mask[h], q, 0.0).astype(BF16)
                s = lax.dot_general(qh, kb, (((1,), (1,)), ((), ())),
                                    preferred_element_type=F32)
                bias = bias_prev_ref[bi, h] if has_prev else bias_first_ref[bi, h]
                s_ref[slot, 2 * u + h, :, :keys] = s + bias

    def softmax(slot, bi, dil, blocks):
        for u, (cls, n, has_prev) in enumerate(blocks):
            keys = 2 * BLK if has_prev else BLK
            row_max = []
            for h in range(2):
                s = s_ref[slot, 2 * u + h, :, :keys]
                m = jnp.max(s, axis=1, keepdims=True)
                p_ref[slot, 2 * u + h, :, :keys] = jnp.exp2(s - m).astype(BF16)
                row_max.append(m)
            scatter(max_ref, bi, pieces(dil, cls, n, False),
                    jnp.where(head_mask[0], row_max[0], row_max[1]))

    def weighted_values(slot, bi, dil, blocks):
        for u, (cls, n, has_prev) in enumerate(blocks):
            keys = 2 * BLK if has_prev else BLK
            v = gather(v_ref, pieces(dil, cls, n, has_prev))
            for h in range(2):
                v_aug = jnp.where(head_mask[h], v, 1.0).astype(BF16)
                scatter(acc_ref, 2 * bi + h, pieces(dil, cls, n, False),
                        jnp.dot(p_ref[slot, 2 * u + h, :, :keys], v_aug,
                                preferred_element_type=F32))

    groups = []
    for bi, dil in enumerate(DILATIONS):
        blocks = [(cls, n, n > 0) for cls in range(dil) for n in range(N_CLASSES // dil)]
        groups += [(bi, dil, blocks[i:i + ATTN_UNROLL])
                   for i in range(0, len(blocks), ATTN_UNROLL)]
    for i in range(len(groups) + 2):
        if i < len(groups):
            scores(i % 2, *groups[i])
        if 1 <= i <= len(groups):
            softmax((i - 1) % 2, *groups[i - 1])
        if i >= 2:
            weighted_values(i % 2, *groups[i - 2])

    for c in range(o_ref.shape[0] // COMBINE_CHUNK):
        rows = pl.ds(c * COMBINE_CHUNK, COMBINE_CHUNK)
        ms = [max_ref[bi, rows, :] for bi in range(len(DILATIONS))]
        m = jnp.maximum(jnp.maximum(ms[0], ms[1]), ms[2])
        tot = [0.0, 0.0]
        for bi in range(len(DILATIONS)):
            w = jnp.exp2(ms[bi] - m)
            w_other = pltpu.roll(w, HEAD_DIM, axis=1)
            tot[0] += jnp.where(head_mask[0], w, w_other) * acc_ref[2 * bi, rows, :]
            tot[1] += jnp.where(head_mask[0], w_other, w) * acc_ref[2 * bi + 1, rows, :]
        num = jnp.where(head_mask[0], tot[0], tot[1])
        den = pltpu.roll(jnp.where(head_mask[0], tot[1], tot[0]), HEAD_DIM, axis=1)
        o_ref[rows, :] = (num / den).astype(o_ref.dtype)


def _attention(qkv, bias_prev, bias_first, batch, seq):
    n_pairs = N_HEADS // 2
    n_slabs = 2 * len(DILATIONS)
    n_dil = len(DILATIONS)
    return pl.pallas_call(
        _attn_kernel,
        grid=(batch, n_pairs),
        in_specs=[pl.BlockSpec((seq, LANES), lambda b, p: (b, p)),
                  pl.BlockSpec((seq, LANES), lambda b, p: (b, n_pairs + p)),
                  pl.BlockSpec((seq, LANES), lambda b, p: (b, 2 * n_pairs + p)),
                  pl.BlockSpec((n_dil, 2, BLK, 2 * BLK), lambda b, p: (0, p, 0, 0)),
                  pl.BlockSpec((n_dil, 2, BLK, BLK), lambda b, p: (0, p, 0, 0))],
        out_specs=pl.BlockSpec((seq, LANES), lambda b, p: (b, p)),
        out_shape=jax.ShapeDtypeStruct((batch * seq, ATTN_WIDTH), BF16),
        scratch_shapes=[pltpu.VMEM((n_slabs, seq, LANES), F32),
                        pltpu.VMEM((n_dil, seq, LANES), F32),
                        pltpu.VMEM((2, 2 * ATTN_UNROLL, BLK, 2 * BLK), F32),
                        pltpu.VMEM((2, 2 * ATTN_UNROLL, BLK, 2 * BLK), BF16)],
        compiler_params=_params(2), name="dilated_attn")(qkv, qkv, qkv, bias_prev, bias_first)


def _conv_kernel(z_ref, gw_ref, cw_ref, cb_ref, lg_ref, lb_ref, o_ref, u_ref, pre_ref):
    cls_rows = z_ref.shape[0] // N_CLASSES
    g0, g1, g2 = 0, GCONV_WIDTH, 2 * GCONV_WIDTH
    c0 = 3 * GCONV_WIDTH
    c1 = c0 + CCONV_WIDTH
    row = lax.broadcasted_iota(jnp.int32, (cls_rows, 1), 0)

    def fill(conv_input, n_shifts):
        def body(r, carry):
            rows = pl.ds(pl.multiple_of(r * cls_rows, cls_rows), cls_rows)
            u = conv_input(rows)
            u_ref[0, rows, :] = u
            for d in range(1, n_shifts):
                u_ref[d, rows, :] = jnp.where(row >= d, pltpu.roll(u, d, axis=0), 0.0)
            return carry
        lax.fori_loop(0, N_CLASSES, body, 0)

    def conv_tile(w_ref, n_taps, r, row0):
        acc = None
        for s in range(n_taps):
            whole, part = divmod(s, N_CLASSES)
            borrow = jnp.where(r < part, 1, 0) if part else 0
            src_cls = r - part + N_CLASSES * borrow
            start = pl.multiple_of(src_cls * cls_rows + row0, CONV_ROWS)
            src = u_ref[whole + borrow, pl.ds(start, CONV_ROWS), :]
            term = w_ref[n_taps - 1 - s:n_taps - s, :] * src
            acc = term if acc is None else acc + term
        return acc

    def tiles(body):
        def step(i, carry):
            r = i // (cls_rows // CONV_ROWS)
            row0 = (i % (cls_rows // CONV_ROWS)) * CONV_ROWS
            body(r, row0, pl.ds(pl.multiple_of(r * cls_rows + row0, CONV_ROWS), CONV_ROWS))
            return carry
        lax.fori_loop(0, N_CLASSES * (cls_rows // CONV_ROWS), step, 0)

    fill(lambda rows: z_ref[rows, g1:g2] * z_ref[rows, g2:c0],
         (GCONV_K - 1) // N_CLASSES + 2)

    def gconv(r, row0, rows):
        acc = conv_tile(gw_ref, GCONV_K, r, row0)
        o_ref[rows, 0:GCONV_WIDTH] = (z_ref[rows, g0:g1] * acc).astype(o_ref.dtype)

    tiles(gconv)

    fill(lambda rows: z_ref[rows, c0:c1] * jax.nn.sigmoid(z_ref[rows, c1:]), CONV_SHIFTS)

    def conv_class(r):
        for row0 in range(0, cls_rows, CONV_ROWS):
            rows = pl.ds(pl.multiple_of(r * cls_rows + row0, CONV_ROWS), CONV_ROWS)
            pre_ref[rows, :] = conv_tile(cw_ref, CCONV_K, r, row0) + cb_ref[...]

    def norm_class(r):
        rows = pl.ds(pl.multiple_of(r * cls_rows, cls_rows), cls_rows)
        u = pre_ref[rows, :]
        mu = jnp.mean(u, axis=-1, keepdims=True)
        d = u - mu
        var = jnp.mean(d * d, axis=-1, keepdims=True)
        y = d * lax.rsqrt(var + LN_EPS) * lg_ref[...] + lb_ref[...]
        o_ref[rows, GCONV_WIDTH:] = (y * jax.nn.sigmoid(y)).astype(o_ref.dtype)

    conv_class(0)

    def step(r, carry):
        norm_class(r - 1)
        conv_class(r)
        return carry

    lax.fori_loop(1, N_CLASSES, step, 0)
    norm_class(N_CLASSES - 1)


def _conv_mixers(rest, gconv_w, cconv_w, cconv_b, cln_g, cln_b, batch, seq):
    width = GCONV_WIDTH + CCONV_WIDTH
    return pl.pallas_call(
        _conv_kernel,
        grid=(batch,),
        in_specs=[pl.BlockSpec((seq, REST_WIDTH), lambda b: (b, 0)),
                  _resident((GCONV_K, GCONV_WIDTH)), _resident((CCONV_K, CCONV_WIDTH)),
                  _resident((1, CCONV_WIDTH)), _resident((1, CCONV_WIDTH)),
                  _resident((1, CCONV_WIDTH))],
        out_specs=pl.BlockSpec((seq, width), lambda b: (b, 0)),
        out_shape=jax.ShapeDtypeStruct((batch * seq, width), BF16),
        scratch_shapes=[pltpu.VMEM((CONV_SHIFTS, seq, GCONV_WIDTH), F32),
                        pltpu.VMEM((seq, CCONV_WIDTH), F32)],
        compiler_params=_params(1), name="conv_mixers")(
            rest, gconv_w, cconv_w, cconv_b.reshape(1, -1), cln_g.reshape(1, -1),
            cln_b.reshape(1, -1))


def kernel(x, w_in, w_out, gconv_w, cconv_w, cconv_b, cln_g, cln_b, ffn1_wg, ffn1_wu, ffn1_wd, ffn2_wg, ffn2_wu, ffn2_wd, norm_ffn1, norm_mix, norm_ffn2, norm_final):
    batch, seq, _ = x.shape
    depth = w_in.shape[0]
    assert seq == N_CLASSES * BLK and x.shape[2] == D_MODEL
    bias_prev, bias_first = (jnp.asarray(t) for t in _alibi_bias_tables())
    ffn1 = [w.astype(BF16) for w in (ffn1_wg, ffn1_wu, ffn1_wd)]
    ffn2 = [w.astype(BF16) for w in (ffn2_wg, ffn2_wu, ffn2_wd)]
    w_in, w_out = w_in.astype(BF16), w_out.astype(BF16)
    h = x
    for l in range(depth):
        last = l == depth - 1
        h = _ffn(h, l, norm_ffn1[l], *ffn1, natural_in=(batch, seq) if l == 0 else None)
        qkv, rest = _inproj(h, l, norm_mix[l], w_in)
        y_attn = _attention(qkv, bias_prev, bias_first, batch, seq)
        y_conv = _conv_mixers(rest, gconv_w[l], cconv_w[l], cconv_b[l], cln_g[l],
                              cln_b[l], batch, seq)
        h = _ffn(h, l, norm_ffn2[l], *ffn2, final_gain=norm_final if last else None,
                 natural_out=(batch, seq) if last else None,
                 mixed=(y_attn, y_conv, w_out))
    return h
```

```python
import functools
import math

import numpy as np
import jax
import jax.numpy as jnp
from jax import lax
from jax.experimental import pallas as pl
from jax.experimental.pallas import tpu as pltpu

D_MODEL = 1024
D_FF = 11 * D_MODEL // 4
HEAD_DIM = 64
ATTN_WIDTH = D_MODEL // 2
N_HEADS = ATTN_WIDTH // HEAD_DIM
GCONV_WIDTH = D_MODEL // 4
CCONV_WIDTH = D_MODEL // 4
GCONV_K = 3
CCONV_K = 31
QKV_WIDTH = 3 * ATTN_WIDTH
REST_WIDTH = 3 * GCONV_WIDTH + 2 * CCONV_WIDTH
D_IN = QKV_WIDTH + REST_WIDTH
DILATIONS = (1, 4, 16)
N_STEPS = 128
BLK = 128
N_CLASSES = max(DILATIONS)
LANES = 128
SUBLANES = 8
NATURAL_CLASSES = SUBLANES
FFN_ROWS = 512
FFN_BLOCK = NATURAL_CLASSES * BLK
RMS_EPS = 1e-6
LN_EPS = 1e-5
LOG2E = math.log2(math.e)
ATTN_UNROLL = 2
PAIRS_PER_STEP = 2
COMBINE_CHUNK = 256
CONV_ROWS = 64
CONV_SHIFTS = (CCONV_K - 1) // N_CLASSES + 2
VMEM_LIMIT = 56 * 1024 * 1024

F32 = jnp.float32
BF16 = jnp.bfloat16


def _rms(x, g):
    return x * lax.rsqrt(jnp.mean(x * x, axis=-1, keepdims=True) + RMS_EPS) * g


def _params(n_axes):
    return pltpu.CompilerParams(
        dimension_semantics=("arbitrary",) * n_axes, vmem_limit_bytes=VMEM_LIMIT)


def _resident(shape, layer=None):
    if layer is None:
        return pl.BlockSpec(shape, lambda *_: (0,) * len(shape),
                            pipeline_mode=pl.Buffered(1))
    return pl.BlockSpec((None,) + shape, lambda *_: (layer,) + (0,) * len(shape),
                        pipeline_mode=pl.Buffered(1))


def _ffn_kernel(x_ref, g_ref, wg_ref, wu_ref, wd_ref, *rest, final, natural_in,
                natural_out, project):
    o_ref = rest[-1]
    cls_per_sub = FFN_ROWS // BLK

    def sub_tile(i):
        rows = pl.ds(pl.multiple_of(i * FFN_ROWS, FFN_ROWS), FFN_ROWS)
        if natural_in:
            x = jnp.concatenate(
                [x_ref[:, i * cls_per_sub + c, :] for c in range(cls_per_sub)], axis=0)
        else:
            x = x_ref[rows, :]
        if project:
            ya_ref, yc_ref, wo_ref = rest[:3]
            x = (x + jnp.dot(ya_ref[rows, :], wo_ref[:ATTN_WIDTH, :],
                             preferred_element_type=F32)
                 + jnp.dot(yc_ref[rows, :], wo_ref[ATTN_WIDTH:, :],
                           preferred_element_type=F32))
        h = _rms(x, g_ref[...]).astype(BF16)
        gate = jnp.dot(h, wg_ref[...], preferred_element_type=F32)
        up = jnp.dot(h, wu_ref[...], preferred_element_type=F32)
        act = (gate * jax.nn.sigmoid(gate) * up).astype(BF16)
        y = x + 0.5 * jnp.dot(act, wd_ref[...], preferred_element_type=F32)
        if final:
            y = _rms(y, rest[-2][...])
        if natural_out:
            for c in range(cls_per_sub):
                o_ref[:, i * cls_per_sub + c, :] = y[c * BLK:(c + 1) * BLK]
        else:
            o_ref[rows, :] = y

    n_sub = (x_ref.shape[1] * BLK if natural_in else x_ref.shape[0]) // FFN_ROWS
    if n_sub == 1:
        sub_tile(0)
    else:
        def step(i, carry):
            sub_tile(i)
            return carry
        lax.fori_loop(0, n_sub, step, 0)


def _ffn(x, layer, gain, wg, wu, wd, final_gain=None, natural_in=None, natural_out=None,
         mixed=None):
    natural = natural_in or natural_out
    if natural:
        batch, seq = natural
        assert seq == N_CLASSES * BLK
        t = batch * seq
        halves = N_CLASSES // NATURAL_CLASSES
        nat_block = pl.BlockSpec((None, BLK, NATURAL_CLASSES, D_MODEL),
                                 lambda b, j: (b, 0, j, 0))
        cls_block = pl.BlockSpec((FFN_BLOCK, D_MODEL),
                                 lambda b, j: (b * halves + j, 0))
        grid = (batch, halves)
        x_spec = nat_block if natural_in else cls_block
        out_spec = nat_block if natural_out else cls_block
        y_spec = pl.BlockSpec((FFN_BLOCK, ATTN_WIDTH),
                              lambda b, j: (b * halves + j, 0))
        if natural_in:
            x = x.reshape(batch, BLK, N_CLASSES, D_MODEL)
    else:
        t = x.shape[0]
        grid = (t // FFN_BLOCK,)
        x_spec = out_spec = pl.BlockSpec((FFN_BLOCK, D_MODEL), lambda i: (i, 0))
        y_spec = pl.BlockSpec((FFN_BLOCK, ATTN_WIDTH), lambda i: (i, 0))
    out_shape = ((batch, BLK, N_CLASSES, D_MODEL) if natural_out else (t, D_MODEL))
    in_specs = [x_spec, _resident((1, D_MODEL)), _resident((D_MODEL, D_FF), layer),
                _resident((D_MODEL, D_FF), layer), _resident((D_FF, D_MODEL), layer)]
    args = [x, gain.reshape(1, D_MODEL), wg, wu, wd]
    if mixed is not None:
        in_specs += [y_spec, y_spec, _resident((D_MODEL, D_MODEL), layer)]
        args += list(mixed)
    if final_gain is not None:
        in_specs.append(_resident((1, D_MODEL)))
        args.append(final_gain.reshape(1, D_MODEL))
    out = pl.pallas_call(
        functools.partial(_ffn_kernel, final=final_gain is not None,
                          natural_in=bool(natural_in), natural_out=bool(natural_out),
                          project=mixed is not None),
        grid=grid, in_specs=in_specs, out_specs=out_spec,
        out_shape=jax.ShapeDtypeStruct(out_shape, F32),
        compiler_params=_params(len(grid)), name="ffn")(*args)
    return out.reshape(batch, seq, D_MODEL) if natural_out else out


def _inproj_kernel(x_ref, g_ref, w_ref, qkv_ref, rest_ref):
    h = _rms(x_ref[...], g_ref[...]).astype(BF16)
    qkv_ref[...] = jnp.dot(h, w_ref[:, :QKV_WIDTH], preferred_element_type=F32)
    rest_ref[...] = jnp.dot(h, w_ref[:, QKV_WIDTH:], preferred_element_type=F32)


def _inproj(x, layer, gain, w_in, tm=1024):
    t = x.shape[0]
    return pl.pallas_call(
        _inproj_kernel, grid=(t // tm,),
        in_specs=[pl.BlockSpec((tm, D_MODEL), lambda i: (i, 0)),
                  _resident((1, D_MODEL)), _resident((D_MODEL, D_IN), layer)],
        out_specs=[pl.BlockSpec((tm, QKV_WIDTH), lambda i: (i, 0)),
                   pl.BlockSpec((tm, REST_WIDTH), lambda i: (i, 0))],
        out_shape=[jax.ShapeDtypeStruct((t, QKV_WIDTH), F32),
                   jax.ShapeDtypeStruct((t, REST_WIDTH), F32)],
        compiler_params=_params(1), name="inproj")(x, gain.reshape(1, D_MODEL), w_in)


def _alibi_bias_tables():
    slopes = np.exp2(-8.0 * np.arange(1, N_HEADS + 1, dtype=np.float64) / N_HEADS)
    with_prev, first = [], []
    for dil in DILATIONS:
        groups = N_CLASSES // dil
        piece = BLK // groups
        q_pos = (groups * np.arange(piece)[None, :] + np.arange(groups)[:, None]).reshape(-1)
        for table, key_piece, shift in ((with_prev, 2 * piece, BLK), (first, piece, 0)):
            k_pos = (groups * np.arange(key_piece)[None, :]
                     + np.arange(groups)[:, None]).reshape(-1) - shift
            steps = q_pos[:, None] - k_pos[None, :]
            valid = (steps >= 0) & (steps <= N_STEPS)
            bias = -slopes[:, None, None] * (steps * dil)[None] * LOG2E
            table.append(np.where(valid[None], bias, -np.inf).astype(np.float32))
    return np.stack(with_prev), np.stack(first)


def _attn_kernel(q_ref, k_ref, v_ref, bias_prev_ref, bias_first_ref, o_ref,
                 acc_ref, max_ref, s_ref, p_ref):
    for pair in range(PAIRS_PER_STEP):
        lanes = pl.ds(pair * LANES, LANES)
        heads = pl.ds(2 * pair, 2)
        _attend_pair(q_ref.at[:, lanes], k_ref.at[:, lanes], v_ref.at[:, lanes],
                     bias_prev_ref.at[:, heads], bias_first_ref.at[:, heads],
                     o_ref.at[:, lanes], acc_ref, max_ref, s_ref, p_ref)


def _attend_pair(q_ref, k_ref, v_ref, bias_prev_ref, bias_first_ref, o_ref,
                 acc_ref, max_ref, s_ref, p_ref):
    lane = lax.broadcasted_iota(jnp.int32, (1, LANES), 1)
    head_mask = (lane < HEAD_DIM, lane >= HEAD_DIM)

    def pieces(dil, cls, n, with_prev):
        groups = N_CLASSES // dil
        piece = BLK // groups
        first = cls * BLK + (n - 1 if with_prev else n) * piece
        rows = 2 * piece if with_prev else piece
        if not isinstance(first, int):
            first = pl.multiple_of(first, SUBLANES)
        return [((dil * g) * BLK + first, rows) for g in range(groups)]

    def gather(ref, where):
        parts = [ref[pl.ds(start, rows), :] for start, rows in where]
        return parts[0] if len(parts) == 1 else jnp.concatenate(parts, axis=0)

    def scatter(ref, slab, where, value):
        offset = 0
        for start, rows in where:
            ref[slab, pl.ds(start, rows), :] = value[offset:offset + rows]
            offset += rows

    def scores(slot, bi, dil, blocks):
        for u, (cls, n, has_prev) in enumerate(blocks):
            keys = 2 * BLK if has_prev else BLK
            q = gather(q_ref, pieces(dil, cls, n, False)) * (HEAD_DIM ** -0.5 * LOG2E)
            kb = gather(k_ref, pieces(dil, cls, n, has_prev)).astype(BF16)
            for h in range(2):
                qh = jnp.where(head_mask[h], q, 0.0).astype(BF16)
                s = lax.dot_general(qh, kb, (((1,), (1,)), ((), ())),
                                    preferred_element_type=F32)
                bias = bias_prev_ref[bi, h] if has_prev else bias_first_ref[bi, h]
                s_ref[slot, 2 * u + h, :, :keys] = s + bias

    def softmax(slot, bi, dil, blocks):
        for u, (cls, n, has_prev) in enumerate(blocks):
            keys = 2 * BLK if has_prev else BLK
            row_max = []
            for h in range(2):
                s = s_ref[slot, 2 * u + h, :, :keys]
                m = jnp.max(s, axis=1, keepdims=True)
                p_ref[slot, 2 * u + h, :, :keys] = jnp.exp2(s - m).astype(BF16)
                row_max.append(m)
            scatter(max_ref, bi, pieces(dil, cls, n, False),
                    jnp.where(head_mask[0], row_max[0], row_max[1]))

    def weighted_values(slot, bi, dil, blocks):
        for u, (cls, n, has_prev) in enumerate(blocks):
            keys = 2 * BLK if has_prev else BLK
            v = gather(v_ref, pieces(dil, cls, n, has_prev))
            for h in range(2):
                v_aug = jnp.where(head_mask[h], v, 1.0).astype(BF16)
                scatter(acc_ref, 2 * bi + h, pieces(dil, cls, n, False),
                        jnp.dot(p_ref[slot, 2 * u + h, :, :keys], v_aug,
                                preferred_element_type=F32))

    groups = []
    for bi, dil in enumerate(DILATIONS):
        blocks = [(cls, n, n > 0) for cls in range(dil) for n in range(N_CLASSES // dil)]
        groups += [(bi, dil, blocks[i:i + ATTN_UNROLL])
                   for i in range(0, len(blocks), ATTN_UNROLL)]
    for i in range(len(groups) + 2):
        if i < len(groups):
            scores(i % 2, *groups[i])
        if 1 <= i <= len(groups):
            softmax((i - 1) % 2, *groups[i - 1])
        if i >= 2:
            weighted_values(i % 2, *groups[i - 2])

    for c in range(o_ref.shape[0] // COMBINE_CHUNK):
        rows = pl.ds(c * COMBINE_CHUNK, COMBINE_CHUNK)
        ms = [max_ref[bi, rows, :] for bi in range(len(DILATIONS))]
        m = jnp.maximum(jnp.maximum(ms[0], ms[1]), ms[2])
        tot = [0.0, 0.0]
        for bi in range(len(DILATIONS)):
            w = jnp.exp2(ms[bi] - m)
            w_other = pltpu.roll(w, HEAD_DIM, axis=1)
            tot[0] += jnp.where(head_mask[0], w, w_other) * acc_ref[2 * bi, rows, :]
            tot[1] += jnp.where(head_mask[0], w_other, w) * acc_ref[2 * bi + 1, rows, :]
        num = jnp.where(head_mask[0], tot[0], tot[1])
        den = pltpu.roll(jnp.where(head_mask[0], tot[1], tot[0]), HEAD_DIM, axis=1)
        o_ref[rows, :] = (num / den).astype(o_ref.dtype)


def _attention(qkv, bias_prev, bias_first, batch, seq):
    steps = N_HEADS // 2 // PAIRS_PER_STEP
    width = PAIRS_PER_STEP * LANES
    n_slabs = 2 * len(DILATIONS)
    n_dil = len(DILATIONS)
    return pl.pallas_call(
        _attn_kernel,
        grid=(batch, steps),
        in_specs=[pl.BlockSpec((seq, width), lambda b, p: (b, p)),
                  pl.BlockSpec((seq, width), lambda b, p: (b, steps + p)),
                  pl.BlockSpec((seq, width), lambda b, p: (b, 2 * steps + p)),
                  pl.BlockSpec((n_dil, 2 * PAIRS_PER_STEP, BLK, 2 * BLK),
                               lambda b, p: (0, p, 0, 0)),
                  pl.BlockSpec((n_dil, 2 * PAIRS_PER_STEP, BLK, BLK),
                               lambda b, p: (0, p, 0, 0))],
        out_specs=pl.BlockSpec((seq, width), lambda b, p: (b, p)),
        out_shape=jax.ShapeDtypeStruct((batch * seq, ATTN_WIDTH), BF16),
        scratch_shapes=[pltpu.VMEM((n_slabs, seq, LANES), F32),
                        pltpu.VMEM((n_dil, seq, LANES), F32),
                        pltpu.VMEM((2, 2 * ATTN_UNROLL, BLK, 2 * BLK), F32),
                        pltpu.VMEM((2, 2 * ATTN_UNROLL, BLK, 2 * BLK), BF16)],
        compiler_params=_params(2), name="dilated_attn")(qkv, qkv, qkv, bias_prev, bias_first)


def _conv_kernel(z_ref, gw_ref, cw_ref, cb_ref, lg_ref, lb_ref, o_ref, u_ref, pre_ref):
    cls_rows = z_ref.shape[0] // N_CLASSES
    g0, g1, g2 = 0, GCONV_WIDTH, 2 * GCONV_WIDTH
    c0 = 3 * GCONV_WIDTH
    c1 = c0 + CCONV_WIDTH
    row = lax.broadcasted_iota(jnp.int32, (cls_rows, 1), 0)

    def fill(conv_input, n_shifts):
        def body(r, carry):
            rows = pl.ds(pl.multiple_of(r * cls_rows, cls_rows), cls_rows)
            u = conv_input(rows)
            u_ref[0, rows, :] = u
            for d in range(1, n_shifts):
                u_ref[d, rows, :] = jnp.where(row >= d, pltpu.roll(u, d, axis=0), 0.0)
            return carry
        lax.fori_loop(0, N_CLASSES, body, 0)

    def conv_tile(w_ref, n_taps, r, row0):
        acc = None
        for s in range(n_taps):
            whole, part = divmod(s, N_CLASSES)
            borrow = jnp.where(r < part, 1, 0) if part else 0
            src_cls = r - part + N_CLASSES * borrow
            start = pl.multiple_of(src_cls * cls_rows + row0, CONV_ROWS)
            src = u_ref[whole + borrow, pl.ds(start, CONV_ROWS), :]
            term = w_ref[n_taps - 1 - s:n_taps - s, :] * src
            acc = term if acc is None else acc + term
        return acc

    def tiles(body):
        def step(i, carry):
            r = i // (cls_rows // CONV_ROWS)
            row0 = (i % (cls_rows // CONV_ROWS)) * CONV_ROWS
            body(r, row0, pl.ds(pl.multiple_of(r * cls_rows + row0, CONV_ROWS), CONV_ROWS))
            return carry
        lax.fori_loop(0, N_CLASSES * (cls_rows // CONV_ROWS), step, 0)

    fill(lambda rows: z_ref[rows, g1:g2] * z_ref[rows, g2:c0],
         (GCONV_K - 1) // N_CLASSES + 2)

    def gconv(r, row0, rows):
        acc = conv_tile(gw_ref, GCONV_K, r, row0)
        o_ref[rows, 0:GCONV_WIDTH] = (z_ref[rows, g0:g1] * acc).astype(o_ref.dtype)

    tiles(gconv)

    fill(lambda rows: z_ref[rows, c0:c1] * jax.nn.sigmoid(z_ref[rows, c1:]), CONV_SHIFTS)

    def conv_class(r):
        for row0 in range(0, cls_rows, CONV_ROWS):
            rows = pl.ds(pl.multiple_of(r * cls_rows + row0, CONV_ROWS), CONV_ROWS)
            pre_ref[rows, :] = conv_tile(cw_ref, CCONV_K, r, row0) + cb_ref[...]

    def norm_class(r):
        rows = pl.ds(pl.multiple_of(r * cls_rows, cls_rows), cls_rows)
        u = pre_ref[rows, :]
        mu = jnp.mean(u, axis=-1, keepdims=True)
        d = u - mu
        var = jnp.mean(d * d, axis=-1, keepdims=True)
        y = d * lax.rsqrt(var + LN_EPS) * lg_ref[...] + lb_ref[...]
        o_ref[rows, GCONV_WIDTH:] = (y * jax.nn.sigmoid(y)).astype(o_ref.dtype)

    conv_class(0)

    def step(r, carry):
        norm_class(r - 1)
        conv_class(r)
        return carry

    lax.fori_loop(1, N_CLASSES, step, 0)
    norm_class(N_CLASSES - 1)


def _conv_mixers(rest, gconv_w, cconv_w, cconv_b, cln_g, cln_b, batch, seq):
    width = GCONV_WIDTH + CCONV_WIDTH
    return pl.pallas_call(
        _conv_kernel,
        grid=(batch,),
        in_specs=[pl.BlockSpec((seq, REST_WIDTH), lambda b: (b, 0)),
                  _resident((GCONV_K, GCONV_WIDTH)), _resident((CCONV_K, CCONV_WIDTH)),
                  _resident((1, CCONV_WIDTH)), _resident((1, CCONV_WIDTH)),
                  _resident((1, CCONV_WIDTH))],
        out_specs=pl.BlockSpec((seq, width), lambda b: (b, 0)),
        out_shape=jax.ShapeDtypeStruct((batch * seq, width), BF16),
        scratch_shapes=[pltpu.VMEM((CONV_SHIFTS, seq, GCONV_WIDTH), F32),
                        pltpu.VMEM((seq, CCONV_WIDTH), F32)],
        compiler_params=_params(1), name="conv_mixers")(
            rest, gconv_w, cconv_w, cconv_b.reshape(1, -1), cln_g.reshape(1, -1),
            cln_b.reshape(1, -1))


def kernel(x, w_in, w_out, gconv_w, cconv_w, cconv_b, cln_g, cln_b, ffn1_wg, ffn1_wu, ffn1_wd, ffn2_wg, ffn2_wu, ffn2_wd, norm_ffn1, norm_mix, norm_ffn2, norm_final):
    batch, seq, _ = x.shape
    depth = w_in.shape[0]
    assert seq == N_CLASSES * BLK and x.shape[2] == D_MODEL
    bias_prev, bias_first = (jnp.asarray(t) for t in _alibi_bias_tables())
    ffn1 = [w.astype(BF16) for w in (ffn1_wg, ffn1_wu, ffn1_wd)]
    ffn2 = [w.astype(BF16) for w in (ffn2_wg, ffn2_wu, ffn2_wd)]
    w_in, w_out = w_in.astype(BF16), w_out.astype(BF16)
    h = x
    for l in range(depth):
        last = l == depth - 1
        h = _ffn(h, l, norm_ffn1[l], *ffn1, natural_in=(batch, seq) if l == 0 else None)
        qkv, rest = _inproj(h, l, norm_mix[l], w_in)
        y_attn = _attention(qkv, bias_prev, bias_first, batch, seq)
        y_conv = _conv_mixers(rest, gconv_w[l], cconv_w[l], cconv_b[l], cln_g[l],
                              cln_b[l], batch, seq)
        h = _ffn(h, l, norm_ffn2[l], *ffn2, final_gain=norm_final if last else None,
                 natural_out=(batch, seq) if last else None,
                 mixed=(y_attn, y_conv, w_out))
    return h
```

```python
import functools
import math

import numpy as np
import jax
import jax.numpy as jnp
from jax import lax
from jax.experimental import pallas as pl
from jax.experimental.pallas import tpu as pltpu

D_MODEL = 1024
D_FF = 11 * D_MODEL // 4
HEAD_DIM = 64
ATTN_WIDTH = D_MODEL // 2
N_HEADS = ATTN_WIDTH // HEAD_DIM
GCONV_WIDTH = D_MODEL // 4
CCONV_WIDTH = D_MODEL // 4
GCONV_K = 3
CCONV_K = 31
QKV_WIDTH = 3 * ATTN_WIDTH
REST_WIDTH = 3 * GCONV_WIDTH + 2 * CCONV_WIDTH
D_IN = QKV_WIDTH + REST_WIDTH
DILATIONS = (1, 4, 16)
N_STEPS = 128
BLK = 128
N_CLASSES = max(DILATIONS)
LANES = 128
SUBLANES = 8
NATURAL_CLASSES = SUBLANES
FFN_ROWS = 512
FFN_BLOCK = NATURAL_CLASSES * BLK
RMS_EPS = 1e-6
LN_EPS = 1e-5
LOG2E = math.log2(math.e)
ATTN_UNROLL = 2
PAIRS_PER_STEP = 2
COMBINE_CHUNK = 256
CONV_ROWS = 64
CONV_SHIFTS = (CCONV_K - 1) // N_CLASSES + 2
VMEM_LIMIT = 56 * 1024 * 1024

F32 = jnp.float32
BF16 = jnp.bfloat16


def _rms(x, g):
    return x * lax.rsqrt(jnp.mean(x * x, axis=-1, keepdims=True) + RMS_EPS) * g


def _params(n_axes):
    return pltpu.CompilerParams(
        dimension_semantics=("arbitrary",) * n_axes, vmem_limit_bytes=VMEM_LIMIT)


def _resident(shape, layer=None):
    if layer is None:
        return pl.BlockSpec(shape, lambda *_: (0,) * len(shape),
                            pipeline_mode=pl.Buffered(1))
    return pl.BlockSpec((None,) + shape, lambda *_: (layer,) + (0,) * len(shape),
                        pipeline_mode=pl.Buffered(1))


def _ffn_kernel(x_ref, g_ref, wg_ref, wu_ref, wd_ref, *rest, final, natural_in,
                natural_out, project):
    o_ref = rest[-1]
    cls_per_sub = FFN_ROWS // BLK

    def prepare(i):
        rows = pl.ds(i * FFN_ROWS, FFN_ROWS)
        if natural_in:
            x = jnp.concatenate(
                [x_ref[:, i * cls_per_sub + c, :] for c in range(cls_per_sub)], axis=0)
        else:
            x = x_ref[rows, :]
        if project:
            ya_ref, yc_ref, wo_ref = rest[:3]
            x = (x + jnp.dot(ya_ref[rows, :], wo_ref[:ATTN_WIDTH, :],
                             preferred_element_type=F32)
                 + jnp.dot(yc_ref[rows, :], wo_ref[ATTN_WIDTH:, :],
                           preferred_element_type=F32))
        return x, _rms(x, g_ref[...]).astype(BF16)

    def finish(i, x, h):
        gate = jnp.dot(h, wg_ref[...], preferred_element_type=F32)
        up = jnp.dot(h, wu_ref[...], preferred_element_type=F32)
        act = (gate * jax.nn.sigmoid(gate) * up).astype(BF16)
        y = x + 0.5 * jnp.dot(act, wd_ref[...], preferred_element_type=F32)
        if final:
            y = _rms(y, rest[-2][...])
        if natural_out:
            for c in range(cls_per_sub):
                o_ref[:, i * cls_per_sub + c, :] = y[c * BLK:(c + 1) * BLK]
        else:
            o_ref[pl.ds(i * FFN_ROWS, FFN_ROWS), :] = y

    n_sub = (x_ref.shape[1] * BLK if natural_in else x_ref.shape[0]) // FFN_ROWS
    prepared = [prepare(i) for i in range(n_sub)]
    for i in range(n_sub):
        finish(i, *prepared[i])


def _ffn(x, layer, gain, wg, wu, wd, final_gain=None, natural_in=None, natural_out=None,
         mixed=None):
    natural = natural_in or natural_out
    if natural:
        batch, seq = natural
        assert seq == N_CLASSES * BLK
        t = batch * seq
        halves = N_CLASSES // NATURAL_CLASSES
        nat_block = pl.BlockSpec((None, BLK, NATURAL_CLASSES, D_MODEL),
                                 lambda b, j: (b, 0, j, 0))
        cls_block = pl.BlockSpec((FFN_BLOCK, D_MODEL),
                                 lambda b, j: (b * halves + j, 0))
        grid = (batch, halves)
        x_spec = nat_block if natural_in else cls_block
        out_spec = nat_block if natural_out else cls_block
        y_spec = pl.BlockSpec((FFN_BLOCK, ATTN_WIDTH),
                              lambda b, j: (b * halves + j, 0))
        if natural_in:
            x = x.reshape(batch, BLK, N_CLASSES, D_MODEL)
    else:
        t = x.shape[0]
        grid = (t // FFN_BLOCK,)
        x_spec = out_spec = pl.BlockSpec((FFN_BLOCK, D_MODEL), lambda i: (i, 0))
        y_spec = pl.BlockSpec((FFN_BLOCK, ATTN_WIDTH), lambda i: (i, 0))
    out_shape = ((batch, BLK, N_CLASSES, D_MODEL) if natural_out else (t, D_MODEL))
    in_specs = [x_spec, _resident((1, D_MODEL)), _resident((D_MODEL, D_FF), layer),
                _resident((D_MODEL, D_FF), layer), _resident((D_FF, D_MODEL), layer)]
    args = [x, gain.reshape(1, D_MODEL), wg, wu, wd]
    if mixed is not None:
        in_specs += [y_spec, y_spec, _resident((D_MODEL, D_MODEL), layer)]
        args += list(mixed)
    if final_gain is not None:
        in_specs.append(_resident((1, D_MODEL)))
        args.append(final_gain.reshape(1, D_MODEL))
    out = pl.pallas_call(
        functools.partial(_ffn_kernel, final=final_gain is not None,
                          natural_in=bool(natural_in), natural_out=bool(natural_out),
                          project=mixed is not None),
        grid=grid, in_specs=in_specs, out_specs=out_spec,
        out_shape=jax.ShapeDtypeStruct(out_shape, F32),
        compiler_params=_params(len(grid)), name="ffn")(*args)
    return out.reshape(batch, seq, D_MODEL) if natural_out else out


def _inproj_kernel(x_ref, g_ref, w_ref, qkv_ref, rest_ref):
    h = _rms(x_ref[...], g_ref[...]).astype(BF16)
    qkv_ref[...] = jnp.dot(h, w_ref[:, :QKV_WIDTH], preferred_element_type=F32)
    rest_ref[...] = jnp.dot(h, w_ref[:, QKV_WIDTH:], preferred_element_type=F32)


def _inproj(x, layer, gain, w_in, tm=1024):
    t = x.shape[0]
    return pl.pallas_call(
        _inproj_kernel, grid=(t // tm,),
        in_specs=[pl.BlockSpec((tm, D_MODEL), lambda i: (i, 0)),
                  _resident((1, D_MODEL)), _resident((D_MODEL, D_IN), layer)],
        out_specs=[pl.BlockSpec((tm, QKV_WIDTH), lambda i: (i, 0)),
                   pl.BlockSpec((tm, REST_WIDTH), lambda i: (i, 0))],
        out_shape=[jax.ShapeDtypeStruct((t, QKV_WIDTH), F32),
                   jax.ShapeDtypeStruct((t, REST_WIDTH), F32)],
        compiler_params=_params(1), name="inproj")(x, gain.reshape(1, D_MODEL), w_in)


def _alibi_bias_tables():
    slopes = np.exp2(-8.0 * np.arange(1, N_HEADS + 1, dtype=np.float64) / N_HEADS)
    with_prev, first = [], []
    for dil in DILATIONS:
        groups = N_CLASSES // dil
        piece = BLK // groups
        q_pos = (groups * np.arange(piece)[None, :] + np.arange(groups)[:, None]).reshape(-1)
        for table, key_piece, shift in ((with_prev, 2 * piece, BLK), (first, piece, 0)):
            k_pos = (groups * np.arange(key_piece)[None, :]
                     + np.arange(groups)[:, None]).reshape(-1) - shift
            steps = q_pos[:, None] - k_pos[None, :]
            valid = (steps >= 0) & (steps <= N_STEPS)
            bias = -slopes[:, None, None] * (steps * dil)[None] * LOG2E
            table.append(np.where(valid[None], bias, -np.inf).astype(np.float32))
    return np.stack(with_prev), np.stack(first)


def _attn_kernel(q_ref, k_ref, v_ref, bias_prev_ref, bias_first_ref, o_ref,
                 acc_ref, max_ref, s_ref, p_ref):
    for pair in range(PAIRS_PER_STEP):
        lanes = pl.ds(pair * LANES, LANES)
        heads = pl.ds(2 * pair, 2)
        _attend_pair(q_ref.at[:, lanes], k_ref.at[:, lanes], v_ref.at[:, lanes],
                     bias_prev_ref.at[:, heads], bias_first_ref.at[:, heads],
                     o_ref.at[:, lanes], acc_ref, max_ref, s_ref, p_ref)


def _attend_pair(q_ref, k_ref, v_ref, bias_prev_ref, bias_first_ref, o_ref,
                 acc_ref, max_ref, s_ref, p_ref):
    lane = lax.broadcasted_iota(jnp.int32, (1, LANES), 1)
    head_mask = (lane < HEAD_DIM, lane >= HEAD_DIM)

    def pieces(dil, cls, n, with_prev):
        groups = N_CLASSES // dil
        piece = BLK // groups
        first = cls * BLK + (n - 1 if with_prev else n) * piece
        rows = 2 * piece if with_prev else piece
        if not isinstance(first, int):
            first = pl.multiple_of(first, SUBLANES)
        return [((dil * g) * BLK + first, rows) for g in range(groups)]

    def gather(ref, where):
        parts = [ref[pl.ds(start, rows), :] for start, rows in where]
        return parts[0] if len(parts) == 1 else jnp.concatenate(parts, axis=0)

    def scatter(ref, slab, where, value):
        offset = 0
        for start, rows in where:
            ref[slab, pl.ds(start, rows), :] = value[offset:offset + rows]
            offset += rows

    def scores(slot, bi, dil, blocks):
        for u, (cls, n, has_prev) in enumerate(blocks):
            keys = 2 * BLK if has_prev else BLK
            q = gather(q_ref, pieces(dil, cls, n, False)) * (HEAD_DIM ** -0.5 * LOG2E)
            kb = gather(k_ref, pieces(dil, cls, n, has_prev)).astype(BF16)
            for h in range(2):
                qh = jnp.where(head_mask[h], q, 0.0).astype(BF16)
                s = lax.dot_general(qh, kb, (((1,), (1,)), ((), ())),
                                    preferred_element_type=F32)
                bias = bias_prev_ref[bi, h] if has_prev else bias_first_ref[bi, h]
                s_ref[slot, 2 * u + h, :, :keys] = s + bias

    def softmax(slot, bi, dil, blocks):
        for u, (cls, n, has_prev) in enumerate(blocks):
            keys = 2 * BLK if has_prev else BLK
            row_max = []
            for h in range(2):
                s = s_ref[slot, 2 * u + h, :, :keys]
                m = jnp.max(s, axis=1, keepdims=True)
                p_ref[slot, 2 * u + h, :, :keys] = jnp.exp2(s - m).astype(BF16)
                row_max.append(m)
            scatter(max_ref, bi, pieces(dil, cls, n, False),
                    jnp.where(head_mask[0], row_max[0], row_max[1]))

    def weighted_values(slot, bi, dil, blocks):
        for u, (cls, n, has_prev) in enumerate(blocks):
            keys = 2 * BLK if has_prev else BLK
            v = gather(v_ref, pieces(dil, cls, n, has_prev))
            for h in range(2):
                v_aug = jnp.where(head_mask[h], v, 1.0).astype(BF16)
                scatter(acc_ref, 2 * bi + h, pieces(dil, cls, n, False),
                        jnp.dot(p_ref[slot, 2 * u + h, :, :keys], v_aug,
                                preferred_element_type=F32))

    groups = []
    for bi, dil in enumerate(DILATIONS):
        blocks = [(cls, n, n > 0) for cls in range(dil) for n in range(N_CLASSES // dil)]
        groups += [(bi, dil, blocks[i:i + ATTN_UNROLL])
                   for i in range(0, len(blocks), ATTN_UNROLL)]
    for i in range(len(groups) + 2):
        if i < len(groups):
            scores(i % 2, *groups[i])
        if 1 <= i <= len(groups):
            softmax((i - 1) % 2, *groups[i - 1])
        if i >= 2:
            weighted_values(i % 2, *groups[i - 2])

    for c in range(o_ref.shape[0] // COMBINE_CHUNK):
        rows = pl.ds(c * COMBINE_CHUNK, COMBINE_CHUNK)
        ms = [max_ref[bi, rows, :] for bi in range(len(DILATIONS))]
        m = jnp.maximum(jnp.maximum(ms[0], ms[1]), ms[2])
        tot = [0.0, 0.0]
        for bi in range(len(DILATIONS)):
            w = jnp.exp2(ms[bi] - m)
            w_other = pltpu.roll(w, HEAD_DIM, axis=1)
            tot[0] += jnp.where(head_mask[0], w, w_other) * acc_ref[2 * bi, rows, :]
            tot[1] += jnp.where(head_mask[0], w_other, w) * acc_ref[2 * bi + 1, rows, :]
        num = jnp.where(head_mask[0], tot[0], tot[1])
        den = pltpu.roll(jnp.where(head_mask[0], tot[1], tot[0]), HEAD_DIM, axis=1)
        o_ref[rows, :] = (num / den).astype(o_ref.dtype)


def _attention(qkv, bias_prev, bias_first, batch, seq):
    steps = N_HEADS // 2 // PAIRS_PER_STEP
    width = PAIRS_PER_STEP * LANES
    n_slabs = 2 * len(DILATIONS)
    n_dil = len(DILATIONS)
    return pl.pallas_call(
        _attn_kernel,
        grid=(batch, steps),
        in_specs=[pl.BlockSpec((seq, width), lambda b, p: (b, p)),
                  pl.BlockSpec((seq, width), lambda b, p: (b, steps + p)),
                  pl.BlockSpec((seq, width), lambda b, p: (b, 2 * steps + p)),
                  pl.BlockSpec((n_dil, 2 * PAIRS_PER_STEP, BLK, 2 * BLK),
                               lambda b, p: (0, p, 0, 0)),
                  pl.BlockSpec((n_dil, 2 * PAIRS_PER_STEP, BLK, BLK),
                               lambda b, p: (0, p, 0, 0))],
        out_specs=pl.BlockSpec((seq, width), lambda b, p: (b, p)),
        out_shape=jax.ShapeDtypeStruct((batch * seq, ATTN_WIDTH), BF16),
        scratch_shapes=[pltpu.VMEM((n_slabs, seq, LANES), F32),
                        pltpu.VMEM((n_dil, seq, LANES), F32),
                        pltpu.VMEM((2, 2 * ATTN_UNROLL, BLK, 2 * BLK), F32),
                        pltpu.VMEM((2, 2 * ATTN_UNROLL, BLK, 2 * BLK), BF16)],
        compiler_params=_params(2), name="dilated_attn")(qkv, qkv, qkv, bias_prev, bias_first)


def _conv_kernel(z_ref, gw_ref, cw_ref, cb_ref, lg_ref, lb_ref, o_ref, u_ref, pre_ref):
    cls_rows = z_ref.shape[0] // N_CLASSES
    g0, g1, g2 = 0, GCONV_WIDTH, 2 * GCONV_WIDTH
    c0 = 3 * GCONV_WIDTH
    c1 = c0 + CCONV_WIDTH
    row = lax.broadcasted_iota(jnp.int32, (cls_rows, 1), 0)

    def fill(conv_input, n_shifts):
        def body(r, carry):
            rows = pl.ds(pl.multiple_of(r * cls_rows, cls_rows), cls_rows)
            u = conv_input(rows)
            u_ref[0, rows, :] = u
            for d in range(1, n_shifts):
                u_ref[d, rows, :] = jnp.where(row >= d, pltpu.roll(u, d, axis=0), 0.0)
            return carry
        lax.fori_loop(0, N_CLASSES, body, 0)

    def conv_tile(w_ref, n_taps, r, row0):
        acc = None
        for s in range(n_taps):
            whole, part = divmod(s, N_CLASSES)
            borrow = jnp.where(r < part, 1, 0) if part else 0
            src_cls = r - part + N_CLASSES * borrow
            start = pl.multiple_of(src_cls * cls_rows + row0, CONV_ROWS)
            src = u_ref[whole + borrow, pl.ds(start, CONV_ROWS), :]
            term = w_ref[n_taps - 1 - s:n_taps - s, :] * src
            acc = term if acc is None else acc + term
        return acc

    def tiles(body):
        def step(i, carry):
            r = i // (cls_rows // CONV_ROWS)
            row0 = (i % (cls_rows // CONV_ROWS)) * CONV_ROWS
            body(r, row0, pl.ds(pl.multiple_of(r * cls_rows + row0, CONV_ROWS), CONV_ROWS))
            return carry
        lax.fori_loop(0, N_CLASSES * (cls_rows // CONV_ROWS), step, 0)

    fill(lambda rows: z_ref[rows, g1:g2] * z_ref[rows, g2:c0],
         (GCONV_K - 1) // N_CLASSES + 2)

    def gconv(r, row0, rows):
        acc = conv_tile(gw_ref, GCONV_K, r, row0)
        o_ref[rows, 0:GCONV_WIDTH] = (z_ref[rows, g0:g1] * acc).astype(o_ref.dtype)

    tiles(gconv)

    fill(lambda rows: z_ref[rows, c0:c1] * jax.nn.sigmoid(z_ref[rows, c1:]), CONV_SHIFTS)

    def conv_class(r):
        for row0 in range(0, cls_rows, CONV_ROWS):
            rows = pl.ds(pl.multiple_of(r * cls_rows + row0, CONV_ROWS), CONV_ROWS)
            pre_ref[rows, :] = conv_tile(cw_ref, CCONV_K, r, row0) + cb_ref[...]

    def norm_class(r):
        rows = pl.ds(pl.multiple_of(r * cls_rows, cls_rows), cls_rows)
        u = pre_ref[rows, :]
        mu = jnp.mean(u, axis=-1, keepdims=True)
        d = u - mu
        var = jnp.mean(d * d, axis=-1, keepdims=True)
        y = d * lax.rsqrt(var + LN_EPS) * lg_ref[...] + lb_ref[...]
        o_ref[rows, GCONV_WIDTH:] = (y * jax.nn.sigmoid(y)).astype(o_ref.dtype)

    conv_class(0)

    def step(r, carry):
        norm_class(r - 1)
        conv_class(r)
        return carry

    lax.fori_loop(1, N_CLASSES, step, 0)
    norm_class(N_CLASSES - 1)


def _conv_mixers(rest, gconv_w, cconv_w, cconv_b, cln_g, cln_b, batch, seq):
    width = GCONV_WIDTH + CCONV_WIDTH
    return pl.pallas_call(
        _conv_kernel,
        grid=(batch,),
        in_specs=[pl.BlockSpec((seq, REST_WIDTH), lambda b: (b, 0)),
                  _resident((GCONV_K, GCONV_WIDTH)), _resident((CCONV_K, CCONV_WIDTH)),
                  _resident((1, CCONV_WIDTH)), _resident((1, CCONV_WIDTH)),
                  _resident((1, CCONV_WIDTH))],
        out_specs=pl.BlockSpec((seq, width), lambda b: (b, 0)),
        out_shape=jax.ShapeDtypeStruct((batch * seq, width), BF16),
        scratch_shapes=[pltpu.VMEM((CONV_SHIFTS, seq, GCONV_WIDTH), F32),
                        pltpu.VMEM((seq, CCONV_WIDTH), F32)],
        compiler_params=_params(1), name="conv_mixers")(
            rest, gconv_w, cconv_w, cconv_b.reshape(1, -1), cln_g.reshape(1, -1),
            cln_b.reshape(1, -1))


def kernel(x, w_in, w_out, gconv_w, cconv_w, cconv_b, cln_g, cln_b, ffn1_wg, ffn1_wu, ffn1_wd, ffn2_wg, ffn2_wu, ffn2_wd, norm_ffn1, norm_mix, norm_ffn2, norm_final):
    batch, seq, _ = x.shape
    depth = w_in.shape[0]
    assert seq == N_CLASSES * BLK and x.shape[2] == D_MODEL
    bias_prev, bias_first = (jnp.asarray(t) for t in _alibi_bias_tables())
    ffn1 = [w.astype(BF16) for w in (ffn1_wg, ffn1_wu, ffn1_wd)]
    ffn2 = [w.astype(BF16) for w in (ffn2_wg, ffn2_wu, ffn2_wd)]
    w_in, w_out = w_in.astype(BF16), w_out.astype(BF16)
    h = x
    for l in range(depth):
        last = l == depth - 1
        h = _ffn(h, l, norm_ffn1[l], *ffn1, natural_in=(batch, seq) if l == 0 else None)
        qkv, rest = _inproj(h, l, norm_mix[l], w_in)
        y_attn = _attention(qkv, bias_prev, bias_first, batch, seq)
        y_conv = _conv_mixers(rest, gconv_w[l], cconv_w[l], cconv_b[l], cln_g[l],
                              cln_b[l], batch, seq)
        h = _ffn(h, l, norm_ffn2[l], *ffn2, final_gain=norm_final if last else None,
                 natural_out=(batch, seq) if last else None,
                 mixed=(y_attn, y_conv, w_out))
    return h
```

```python
import functools
import math

import numpy as np
import jax
import jax.numpy as jnp
from jax import lax
from jax.experimental import pallas as pl
from jax.experimental.pallas import tpu as pltpu

D_MODEL = 1024
D_FF = 11 * D_MODEL // 4
HEAD_DIM = 64
ATTN_WIDTH = D_MODEL // 2
N_HEADS = ATTN_WIDTH // HEAD_DIM
GCONV_WIDTH = D_MODEL // 4
CCONV_WIDTH = D_MODEL // 4
GCONV_K = 3
CCONV_K = 31
QKV_WIDTH = 3 * ATTN_WIDTH
REST_WIDTH = 3 * GCONV_WIDTH + 2 * CCONV_WIDTH
D_IN = QKV_WIDTH + REST_WIDTH
DILATIONS = (1, 4, 16)
N_STEPS = 128
BLK = 128
N_CLASSES = max(DILATIONS)
LANES = 128
SUBLANES = 8
NATURAL_CLASSES = SUBLANES
FFN_ROWS = 256
PROJ_ROWS = 512
FFN_BLOCK = NATURAL_CLASSES * BLK
RMS_EPS = 1e-6
LN_EPS = 1e-5
LOG2E = math.log2(math.e)
ATTN_UNROLL = 2
PAIRS_PER_STEP = 2
COMBINE_CHUNK = 256
CONV_ROWS = 64
CONV_SHIFTS = (CCONV_K - 1) // N_CLASSES + 2
VMEM_LIMIT = 56 * 1024 * 1024

F32 = jnp.float32
BF16 = jnp.bfloat16


def _rms(x, g):
    return x * lax.rsqrt(jnp.mean(x * x, axis=-1, keepdims=True) + RMS_EPS) * g


def _params(n_axes):
    return pltpu.CompilerParams(
        dimension_semantics=("arbitrary",) * n_axes, vmem_limit_bytes=VMEM_LIMIT)


def _resident(shape, layer=None):
    if layer is None:
        return pl.BlockSpec(shape, lambda *_: (0,) * len(shape),
                            pipeline_mode=pl.Buffered(1))
    return pl.BlockSpec((None,) + shape, lambda *_: (layer,) + (0,) * len(shape),
                        pipeline_mode=pl.Buffered(1))


def _ffn_kernel(x_ref, g_ref, wg_ref, wu_ref, wd_ref, *rest, final, natural_in,
                natural_out, project):
    o_ref = rest[-1]
    cls_per_sub = FFN_ROWS // BLK

    def prepare(i):
        rows = pl.ds(i * FFN_ROWS, FFN_ROWS)
        if natural_in:
            x = jnp.concatenate(
                [x_ref[:, i * cls_per_sub + c, :] for c in range(cls_per_sub)], axis=0)
        else:
            x = x_ref[rows, :]
        if project:
            ya_ref, yc_ref, wo_ref = rest[:3]
            x = (x + jnp.dot(ya_ref[rows, :], wo_ref[:ATTN_WIDTH, :],
                             preferred_element_type=F32)
                 + jnp.dot(yc_ref[rows, :], wo_ref[ATTN_WIDTH:, :],
                           preferred_element_type=F32))
        return x, _rms(x, g_ref[...]).astype(BF16)

    def finish(i, x, h):
        gate = jnp.dot(h, wg_ref[...], preferred_element_type=F32)
        up = jnp.dot(h, wu_ref[...], preferred_element_type=F32)
        act = (gate * jax.nn.sigmoid(gate) * up).astype(BF16)
        y = x + 0.5 * jnp.dot(act, wd_ref[...], preferred_element_type=F32)
        if final:
            y = _rms(y, rest[-2][...])
        if natural_out:
            for c in range(cls_per_sub):
                o_ref[:, i * cls_per_sub + c, :] = y[c * BLK:(c + 1) * BLK]
        else:
            o_ref[pl.ds(i * FFN_ROWS, FFN_ROWS), :] = y

    n_sub = (x_ref.shape[1] * BLK if natural_in else x_ref.shape[0]) // FFN_ROWS
    prepared = [prepare(i) for i in range(n_sub)]
    for i in range(n_sub):
        finish(i, *prepared[i])


def _ffn(x, layer, gain, wg, wu, wd, final_gain=None, natural_in=None, natural_out=None,
         mixed=None):
    natural = natural_in or natural_out
    if natural:
        batch, seq = natural
        assert seq == N_CLASSES * BLK
        t = batch * seq
        halves = N_CLASSES // NATURAL_CLASSES
        nat_block = pl.BlockSpec((None, BLK, NATURAL_CLASSES, D_MODEL),
                                 lambda b, j: (b, 0, j, 0))
        cls_block = pl.BlockSpec((FFN_BLOCK, D_MODEL),
                                 lambda b, j: (b * halves + j, 0))
        grid = (batch, halves)
        x_spec = nat_block if natural_in else cls_block
        out_spec = nat_block if natural_out else cls_block
        y_spec = pl.BlockSpec((FFN_BLOCK, ATTN_WIDTH),
                              lambda b, j: (b * halves + j, 0))
        if natural_in:
            x = x.reshape(batch, BLK, N_CLASSES, D_MODEL)
    else:
        t = x.shape[0]
        grid = (t // FFN_BLOCK,)
        x_spec = out_spec = pl.BlockSpec((FFN_BLOCK, D_MODEL), lambda i: (i, 0))
        y_spec = pl.BlockSpec((FFN_BLOCK, ATTN_WIDTH), lambda i: (i, 0))
    out_shape = ((batch, BLK, N_CLASSES, D_MODEL) if natural_out else (t, D_MODEL))
    in_specs = [x_spec, _resident((1, D_MODEL)), _resident((D_MODEL, D_FF), layer),
                _resident((D_MODEL, D_FF), layer), _resident((D_FF, D_MODEL), layer)]
    args = [x, gain.reshape(1, D_MODEL), wg, wu, wd]
    if mixed is not None:
        in_specs += [y_spec, y_spec, _resident((D_MODEL, D_MODEL), layer)]
        args += list(mixed)
    if final_gain is not None:
        in_specs.append(_resident((1, D_MODEL)))
        args.append(final_gain.reshape(1, D_MODEL))
    out = pl.pallas_call(
        functools.partial(_ffn_kernel, final=final_gain is not None,
                          natural_in=bool(natural_in), natural_out=bool(natural_out),
                          project=mixed is not None),
        grid=grid, in_specs=in_specs, out_specs=out_spec,
        out_shape=jax.ShapeDtypeStruct(out_shape, F32),
        compiler_params=_params(len(grid)), name="ffn")(*args)
    return out.reshape(batch, seq, D_MODEL) if natural_out else out


def _inproj_kernel(x_ref, g_ref, w_ref, qkv_ref, rest_ref):
    n_sub = x_ref.shape[0] // PROJ_ROWS
    hs = [_rms(x_ref[i * PROJ_ROWS:(i + 1) * PROJ_ROWS, :], g_ref[...]).astype(BF16)
          for i in range(n_sub)]
    for i, h in enumerate(hs):
        rows = slice(i * PROJ_ROWS, (i + 1) * PROJ_ROWS)
        qkv_ref[rows, :] = jnp.dot(h, w_ref[:, :QKV_WIDTH], preferred_element_type=F32)
        rest_ref[rows, :] = jnp.dot(h, w_ref[:, QKV_WIDTH:], preferred_element_type=F32)


def _inproj(x, layer, gain, w_in, tm=1024):
    t = x.shape[0]
    return pl.pallas_call(
        _inproj_kernel, grid=(t // tm,),
        in_specs=[pl.BlockSpec((tm, D_MODEL), lambda i: (i, 0)),
                  _resident((1, D_MODEL)), _resident((D_MODEL, D_IN), layer)],
        out_specs=[pl.BlockSpec((tm, QKV_WIDTH), lambda i: (i, 0)),
                   pl.BlockSpec((tm, REST_WIDTH), lambda i: (i, 0))],
        out_shape=[jax.ShapeDtypeStruct((t, QKV_WIDTH), F32),
                   jax.ShapeDtypeStruct((t, REST_WIDTH), F32)],
        compiler_params=_params(1), name="inproj")(x, gain.reshape(1, D_MODEL), w_in)


def _alibi_bias_tables():
    slopes = np.exp2(-8.0 * np.arange(1, N_HEADS + 1, dtype=np.float64) / N_HEADS)
    with_prev, first = [], []
    for dil in DILATIONS:
        groups = N_CLASSES // dil
        piece = BLK // groups
        q_pos = (groups * np.arange(piece)[None, :] + np.arange(groups)[:, None]).reshape(-1)
        for table, key_piece, shift in ((with_prev, 2 * piece, BLK), (first, piece, 0)):
            k_pos = (groups * np.arange(key_piece)[None, :]
                     + np.arange(groups)[:, None]).reshape(-1) - shift
            steps = q_pos[:, None] - k_pos[None, :]
            valid = (steps >= 0) & (steps <= N_STEPS)
            bias = -slopes[:, None, None] * (steps * dil)[None] * LOG2E
            table.append(np.where(valid[None], bias, -np.inf).astype(np.float32))
    return np.stack(with_prev), np.stack(first)


def _attn_kernel(q_ref, k_ref, v_ref, bias_prev_ref, bias_first_ref, o_ref,
                 acc_ref, max_ref, s_ref, p_ref):
    for pair in range(PAIRS_PER_STEP):
        lanes = pl.ds(pair * LANES, LANES)
        heads = pl.ds(2 * pair, 2)
        _attend_pair(q_ref.at[:, lanes], k_ref.at[:, lanes], v_ref.at[:, lanes],
                     bias_prev_ref.at[:, heads], bias_first_ref.at[:, heads],
                     o_ref.at[:, lanes], acc_ref, max_ref, s_ref, p_ref)


def _attend_pair(q_ref, k_ref, v_ref, bias_prev_ref, bias_first_ref, o_ref,
                 acc_ref, max_ref, s_ref, p_ref):
    lane = lax.broadcasted_iota(jnp.int32, (1, LANES), 1)
    head_mask = (lane < HEAD_DIM, lane >= HEAD_DIM)

    def pieces(dil, cls, n, with_prev):
        groups = N_CLASSES // dil
        piece = BLK // groups
        first = cls * BLK + (n - 1 if with_prev else n) * piece
        rows = 2 * piece if with_prev else piece
        if not isinstance(first, int):
            first = pl.multiple_of(first, SUBLANES)
        return [((dil * g) * BLK + first, rows) for g in range(groups)]

    def gather(ref, where):
        parts = [ref[pl.ds(start, rows), :] for start, rows in where]
        return parts[0] if len(parts) == 1 else jnp.concatenate(parts, axis=0)

    def scatter(ref, slab, where, value):
        offset = 0
        for start, rows in where:
            ref[slab, pl.ds(start, rows), :] = value[offset:offset + rows]
            offset += rows

    def scores(slot, bi, dil, blocks):
        for u, (cls, n, has_prev) in enumerate(blocks):
            keys = 2 * BLK if has_prev else BLK
            q = gather(q_ref, pieces(dil, cls, n, False)) * (HEAD_DIM ** -0.5 * LOG2E)
            kb = gather(k_ref, pieces(dil, cls, n, has_prev)).astype(BF16)
            for h in range(2):
                qh = jnp.where(head_mask[h], q, 0.0).astype(BF16)
                s = lax.dot_general(qh, kb, (((1,), (1,)), ((), ())),
                                    preferred_element_type=F32)
                bias = bias_prev_ref[bi, h] if has_prev else bias_first_ref[bi, h]
                s_ref[slot, 2 * u + h, :, :keys] = s + bias

    def softmax(slot, bi, dil, blocks):
        for u, (cls, n, has_prev) in enumerate(blocks):
            keys = 2 * BLK if has_prev else BLK
            row_max = []
            for h in range(2):
                s = s_ref[slot, 2 * u + h, :, :keys]
                m = jnp.max(s, axis=1, keepdims=True)
                p_ref[slot, 2 * u + h, :, :keys] = jnp.exp2(s - m).astype(BF16)
                row_max.append(m)
            scatter(max_ref, bi, pieces(dil, cls, n, False),
                    jnp.where(head_mask[0], row_max[0], row_max[1]))

    def weighted_values(slot, bi, dil, blocks):
        for u, (cls, n, has_prev) in enumerate(blocks):
            keys = 2 * BLK if has_prev else BLK
            v = gather(v_ref, pieces(dil, cls, n, has_prev))
            for h in range(2):
                v_aug = jnp.where(head_mask[h], v, 1.0).astype(BF16)
                scatter(acc_ref, 2 * bi + h, pieces(dil, cls, n, False),
                        jnp.dot(p_ref[slot, 2 * u + h, :, :keys], v_aug,
                                preferred_element_type=F32))

    groups = []
    for bi, dil in enumerate(DILATIONS):
        blocks = [(cls, n, n > 0) for cls in range(dil) for n in range(N_CLASSES // dil)]
        groups += [(bi, dil, blocks[i:i + ATTN_UNROLL])
                   for i in range(0, len(blocks), ATTN_UNROLL)]
    for i in range(len(groups) + 2):
        if i < len(groups):
            scores(i % 2, *groups[i])
        if 1 <= i <= len(groups):
            softmax((i - 1) % 2, *groups[i - 1])
        if i >= 2:
            weighted_values(i % 2, *groups[i - 2])

    for c in range(o_ref.shape[0] // COMBINE_CHUNK):
        rows = pl.ds(c * COMBINE_CHUNK, COMBINE_CHUNK)
        ms = [max_ref[bi, rows, :] for bi in range(len(DILATIONS))]
        m = jnp.maximum(jnp.maximum(ms[0], ms[1]), ms[2])
        tot = [0.0, 0.0]
        for bi in range(len(DILATIONS)):
            w = jnp.exp2(ms[bi] - m)
            w_other = pltpu.roll(w, HEAD_DIM, axis=1)
            tot[0] += jnp.where(head_mask[0], w, w_other) * acc_ref[2 * bi, rows, :]
            tot[1] += jnp.where(head_mask[0], w_other, w) * acc_ref[2 * bi + 1, rows, :]
        num = jnp.where(head_mask[0], tot[0], tot[1])
        den = pltpu.roll(jnp.where(head_mask[0], tot[1], tot[0]), HEAD_DIM, axis=1)
        o_ref[rows, :] = (num / den).astype(o_ref.dtype)


def _attention(qkv, bias_prev, bias_first, batch, seq):
    steps = N_HEADS // 2 // PAIRS_PER_STEP
    width = PAIRS_PER_STEP * LANES
    n_slabs = 2 * len(DILATIONS)
    n_dil = len(DILATIONS)
    return pl.pallas_call(
        _attn_kernel,
        grid=(batch, steps),
        in_specs=[pl.BlockSpec((seq, width), lambda b, p: (b, p)),
                  pl.BlockSpec((seq, width), lambda b, p: (b, steps + p)),
                  pl.BlockSpec((seq, width), lambda b, p: (b, 2 * steps + p)),
                  pl.BlockSpec((n_dil, 2 * PAIRS_PER_STEP, BLK, 2 * BLK),
                               lambda b, p: (0, p, 0, 0)),
                  pl.BlockSpec((n_dil, 2 * PAIRS_PER_STEP, BLK, BLK),
                               lambda b, p: (0, p, 0, 0))],
        out_specs=pl.BlockSpec((seq, width), lambda b, p: (b, p)),
        out_shape=jax.ShapeDtypeStruct((batch * seq, ATTN_WIDTH), BF16),
        scratch_shapes=[pltpu.VMEM((n_slabs, seq, LANES), F32),
                        pltpu.VMEM((n_dil, seq, LANES), F32),
                        pltpu.VMEM((2, 2 * ATTN_UNROLL, BLK, 2 * BLK), F32),
                        pltpu.VMEM((2, 2 * ATTN_UNROLL, BLK, 2 * BLK), BF16)],
        compiler_params=_params(2), name="dilated_attn")(qkv, qkv, qkv, bias_prev, bias_first)


def _conv_kernel(z_ref, gw_ref, cw_ref, cb_ref, lg_ref, lb_ref, o_ref, u_ref, pre_ref):
    cls_rows = z_ref.shape[0] // N_CLASSES
    g0, g1, g2 = 0, GCONV_WIDTH, 2 * GCONV_WIDTH
    c0 = 3 * GCONV_WIDTH
    c1 = c0 + CCONV_WIDTH
    row = lax.broadcasted_iota(jnp.int32, (cls_rows, 1), 0)

    def fill(conv_input, n_shifts):
        def body(r, carry):
            rows = pl.ds(pl.multiple_of(r * cls_rows, cls_rows), cls_rows)
            u = conv_input(rows)
            u_ref[0, rows, :] = u
            for d in range(1, n_shifts):
                u_ref[d, rows, :] = jnp.where(row >= d, pltpu.roll(u, d, axis=0), 0.0)
            return carry
        lax.fori_loop(0, N_CLASSES, body, 0)

    def conv_tile(w_ref, n_taps, r, row0):
        acc = None
        for s in range(n_taps):
            whole, part = divmod(s, N_CLASSES)
            borrow = jnp.where(r < part, 1, 0) if part else 0
            src_cls = r - part + N_CLASSES * borrow
            start = pl.multiple_of(src_cls * cls_rows + row0, CONV_ROWS)
            src = u_ref[whole + borrow, pl.ds(start, CONV_ROWS), :]
            term = w_ref[n_taps - 1 - s:n_taps - s, :] * src
            acc = term if acc is None else acc + term
        return acc

    def tiles(body):
        def step(i, carry):
            r = i // (cls_rows // CONV_ROWS)
            row0 = (i % (cls_rows // CONV_ROWS)) * CONV_ROWS
            body(r, row0, pl.ds(pl.multiple_of(r * cls_rows + row0, CONV_ROWS), CONV_ROWS))
            return carry
        lax.fori_loop(0, N_CLASSES * (cls_rows // CONV_ROWS), step, 0)

    fill(lambda rows: z_ref[rows, g1:g2] * z_ref[rows, g2:c0],
         (GCONV_K - 1) // N_CLASSES + 2)

    def gconv(r, row0, rows):
        acc = conv_tile(gw_ref, GCONV_K, r, row0)
        o_ref[rows, 0:GCONV_WIDTH] = (z_ref[rows, g0:g1] * acc).astype(o_ref.dtype)

    tiles(gconv)

    fill(lambda rows: z_ref[rows, c0:c1] * jax.nn.sigmoid(z_ref[rows, c1:]), CONV_SHIFTS)

    def conv_class(r):
        for row0 in range(0, cls_rows, CONV_ROWS):
            rows = pl.ds(pl.multiple_of(r * cls_rows + row0, CONV_ROWS), CONV_ROWS)
            pre_ref[rows, :] = conv_tile(cw_ref, CCONV_K, r, row0) + cb_ref[...]

    def norm_class(r):
        rows = pl.ds(pl.multiple_of(r * cls_rows, cls_rows), cls_rows)
        u = pre_ref[rows, :]
        mu = jnp.mean(u, axis=-1, keepdims=True)
        d = u - mu
        var = jnp.mean(d * d, axis=-1, keepdims=True)
        y = d * lax.rsqrt(var + LN_EPS) * lg_ref[...] + lb_ref[...]
        o_ref[rows, GCONV_WIDTH:] = (y * jax.nn.sigmoid(y)).astype(o_ref.dtype)

    conv_class(0)

    def step(r, carry):
        norm_class(r - 1)
        conv_class(r)
        return carry

    lax.fori_loop(1, N_CLASSES, step, 0)
    norm_class(N_CLASSES - 1)


def _conv_mixers(rest, gconv_w, cconv_w, cconv_b, cln_g, cln_b, batch, seq):
    width = GCONV_WIDTH + CCONV_WIDTH
    return pl.pallas_call(
        _conv_kernel,
        grid=(batch,),
        in_specs=[pl.BlockSpec((seq, REST_WIDTH), lambda b: (b, 0)),
                  _resident((GCONV_K, GCONV_WIDTH)), _resident((CCONV_K, CCONV_WIDTH)),
                  _resident((1, CCONV_WIDTH)), _resident((1, CCONV_WIDTH)),
                  _resident((1, CCONV_WIDTH))],
        out_specs=pl.BlockSpec((seq, width), lambda b: (b, 0)),
        out_shape=jax.ShapeDtypeStruct((batch * seq, width), BF16),
        scratch_shapes=[pltpu.VMEM((CONV_SHIFTS, seq, GCONV_WIDTH), F32),
                        pltpu.VMEM((seq, CCONV_WIDTH), F32)],
        compiler_params=_params(1), name="conv_mixers")(
            rest, gconv_w, cconv_w, cconv_b.reshape(1, -1), cln_g.reshape(1, -1),
            cln_b.reshape(1, -1))


def kernel(x, w_in, w_out, gconv_w, cconv_w, cconv_b, cln_g, cln_b, ffn1_wg, ffn1_wu, ffn1_wd, ffn2_wg, ffn2_wu, ffn2_wd, norm_ffn1, norm_mix, norm_ffn2, norm_final):
    batch, seq, _ = x.shape
    depth = w_in.shape[0]
    assert seq == N_CLASSES * BLK and x.shape[2] == D_MODEL
    bias_prev, bias_first = (jnp.asarray(t) for t in _alibi_bias_tables())
    ffn1 = [w.astype(BF16) for w in (ffn1_wg, ffn1_wu, ffn1_wd)]
    ffn2 = [w.astype(BF16) for w in (ffn2_wg, ffn2_wu, ffn2_wd)]
    w_in, w_out = w_in.astype(BF16), w_out.astype(BF16)
    h = x
    for l in range(depth):
        last = l == depth - 1
        h = _ffn(h, l, norm_ffn1[l], *ffn1, natural_in=(batch, seq) if l == 0 else None)
        qkv, rest = _inproj(h, l, norm_mix[l], w_in)
        y_attn = _attention(qkv, bias_prev, bias_first, batch, seq)
        y_conv = _conv_mixers(rest, gconv_w[l], cconv_w[l], cconv_b[l], cln_g[l],
                              cln_b[l], batch, seq)
        h = _ffn(h, l, norm_ffn2[l], *ffn2, final_gain=norm_final if last else None,
                 natural_out=(batch, seq) if last else None,
                 mixed=(y_attn, y_conv, w_out))
    return h
```

```python
import functools
import math

import numpy as np
import jax
import jax.numpy as jnp
from jax import lax
from jax.experimental import pallas as pl
from jax.experimental.pallas import tpu as pltpu

D_MODEL = 1024
D_FF = 11 * D_MODEL // 4
HEAD_DIM = 64
ATTN_WIDTH = D_MODEL // 2
N_HEADS = ATTN_WIDTH // HEAD_DIM
GCONV_WIDTH = D_MODEL // 4
CCONV_WIDTH = D_MODEL // 4
GCONV_K = 3
CCONV_K = 31
QKV_WIDTH = 3 * ATTN_WIDTH
REST_WIDTH = 3 * GCONV_WIDTH + 2 * CCONV_WIDTH
D_IN = QKV_WIDTH + REST_WIDTH
DILATIONS = (1, 4, 16)
N_STEPS = 128
BLK = 128
N_CLASSES = max(DILATIONS)
LANES = 128
SUBLANES = 8
NATURAL_CLASSES = SUBLANES
FFN_ROWS = 256
PROJ_ROWS = 512
FFN_BLOCK = NATURAL_CLASSES * BLK
RMS_EPS = 1e-6
LN_EPS = 1e-5
LOG2E = math.log2(math.e)
ATTN_UNROLL = 2
PAIRS_PER_STEP = 2
COMBINE_CHUNK = 256
CONV_ROWS = 64
CONV_SHIFTS = (CCONV_K - 1) // N_CLASSES + 2
VMEM_LIMIT = 56 * 1024 * 1024

F32 = jnp.float32
BF16 = jnp.bfloat16


def _rms(x, g):
    return x * lax.rsqrt(jnp.mean(x * x, axis=-1, keepdims=True) + RMS_EPS) * g


def _params(n_axes):
    return pltpu.CompilerParams(
        dimension_semantics=("arbitrary",) * n_axes, vmem_limit_bytes=VMEM_LIMIT)


def _resident(shape, layer=None):
    if layer is None:
        return pl.BlockSpec(shape, lambda *_: (0,) * len(shape),
                            pipeline_mode=pl.Buffered(1))
    return pl.BlockSpec((None,) + shape, lambda *_: (layer,) + (0,) * len(shape),
                        pipeline_mode=pl.Buffered(1))


def _ffn_kernel(x_ref, g_ref, wg_ref, wu_ref, wd_ref, *rest, final, natural_in,
                natural_out, project):
    o_ref = rest[-1]
    cls_per_sub = FFN_ROWS // BLK

    def prepare(i):
        rows = pl.ds(i * FFN_ROWS, FFN_ROWS)
        if natural_in:
            x = jnp.concatenate(
                [x_ref[:, i * cls_per_sub + c, :] for c in range(cls_per_sub)], axis=0)
        else:
            x = x_ref[rows, :]
        if project:
            ya_ref, yc_ref, wo_ref = rest[:3]
            x = (x + jnp.dot(ya_ref[rows, :], wo_ref[:ATTN_WIDTH, :],
                             preferred_element_type=F32)
                 + jnp.dot(yc_ref[rows, :], wo_ref[ATTN_WIDTH:, :],
                           preferred_element_type=F32))
        return x, _rms(x, g_ref[...]).astype(BF16)

    def finish(i, x, h):
        gate = jnp.dot(h, wg_ref[...], preferred_element_type=F32)
        up = jnp.dot(h, wu_ref[...], preferred_element_type=F32)
        act = (gate * jax.nn.sigmoid(gate) * up).astype(BF16)
        y = x + 0.5 * jnp.dot(act, wd_ref[...], preferred_element_type=F32)
        if final:
            y = _rms(y, rest[-2][...])
        if natural_out:
            for c in range(cls_per_sub):
                o_ref[:, i * cls_per_sub + c, :] = y[c * BLK:(c + 1) * BLK]
        else:
            o_ref[pl.ds(i * FFN_ROWS, FFN_ROWS), :] = y

    n_sub = (x_ref.shape[1] * BLK if natural_in else x_ref.shape[0]) // FFN_ROWS
    prepared = [prepare(i) for i in range(n_sub)]
    for i in range(n_sub):
        finish(i, *prepared[i])


def _ffn(x, layer, gain, wg, wu, wd, final_gain=None, natural_in=None, natural_out=None,
         mixed=None):
    natural = natural_in or natural_out
    if natural:
        batch, seq = natural
        assert seq == N_CLASSES * BLK
        t = batch * seq
        halves = N_CLASSES // NATURAL_CLASSES
        nat_block = pl.BlockSpec((None, BLK, NATURAL_CLASSES, D_MODEL),
                                 lambda b, j: (b, 0, j, 0))
        cls_block = pl.BlockSpec((FFN_BLOCK, D_MODEL),
                                 lambda b, j: (b * halves + j, 0))
        grid = (batch, halves)
        x_spec = nat_block if natural_in else cls_block
        out_spec = nat_block if natural_out else cls_block
        y_spec = pl.BlockSpec((FFN_BLOCK, ATTN_WIDTH),
                              lambda b, j: (b * halves + j, 0))
        if natural_in:
            x = x.reshape(batch, BLK, N_CLASSES, D_MODEL)
    else:
        t = x.shape[0]
        grid = (t // FFN_BLOCK,)
        x_spec = out_spec = pl.BlockSpec((FFN_BLOCK, D_MODEL), lambda i: (i, 0))
        y_spec = pl.BlockSpec((FFN_BLOCK, ATTN_WIDTH), lambda i: (i, 0))
    out_shape = ((batch, BLK, N_CLASSES, D_MODEL) if natural_out else (t, D_MODEL))
    in_specs = [x_spec, _resident((1, D_MODEL)), _resident((D_MODEL, D_FF), layer),
                _resident((D_MODEL, D_FF), layer), _resident((D_FF, D_MODEL), layer)]
    args = [x, gain.reshape(1, D_MODEL), wg, wu, wd]
    if mixed is not None:
        in_specs += [y_spec, y_spec, _resident((D_MODEL, D_MODEL), layer)]
        args += list(mixed)
    if final_gain is not None:
        in_specs.append(_resident((1, D_MODEL)))
        args.append(final_gain.reshape(1, D_MODEL))
    out = pl.pallas_call(
        functools.partial(_ffn_kernel, final=final_gain is not None,
                          natural_in=bool(natural_in), natural_out=bool(natural_out),
                          project=mixed is not None),
        grid=grid, in_specs=in_specs, out_specs=out_spec,
        out_shape=jax.ShapeDtypeStruct(out_shape, F32),
        compiler_params=_params(len(grid)), name="ffn")(*args)
    return out.reshape(batch, seq, D_MODEL) if natural_out else out


def _inproj_kernel(x_ref, g_ref, w_ref, qkv_ref, rest_ref):
    n_sub = x_ref.shape[0] // PROJ_ROWS
    hs = [_rms(x_ref[i * PROJ_ROWS:(i + 1) * PROJ_ROWS, :], g_ref[...]).astype(BF16)
          for i in range(n_sub)]
    for i, h in enumerate(hs):
        rows = slice(i * PROJ_ROWS, (i + 1) * PROJ_ROWS)
        qkv_ref[rows, :] = jnp.dot(h, w_ref[:, :QKV_WIDTH], preferred_element_type=F32)
        rest_ref[rows, :] = jnp.dot(h, w_ref[:, QKV_WIDTH:], preferred_element_type=F32)


def _inproj(x, layer, gain, w_in, tm=1024):
    t = x.shape[0]
    return pl.pallas_call(
        _inproj_kernel, grid=(t // tm,),
        in_specs=[pl.BlockSpec((tm, D_MODEL), lambda i: (i, 0)),
                  _resident((1, D_MODEL)), _resident((D_MODEL, D_IN), layer)],
        out_specs=[pl.BlockSpec((tm, QKV_WIDTH), lambda i: (i, 0)),
                   pl.BlockSpec((tm, REST_WIDTH), lambda i: (i, 0))],
        out_shape=[jax.ShapeDtypeStruct((t, QKV_WIDTH), F32),
                   jax.ShapeDtypeStruct((t, REST_WIDTH), F32)],
        compiler_params=_params(1), name="inproj")(x, gain.reshape(1, D_MODEL), w_in)


def _alibi_bias_tables():
    slopes = np.exp2(-8.0 * np.arange(1, N_HEADS + 1, dtype=np.float64) / N_HEADS)
    with_prev, first = [], []
    for dil in DILATIONS:
        groups = N_CLASSES // dil
        piece = BLK // groups
        q_pos = (groups * np.arange(piece)[None, :] + np.arange(groups)[:, None]).reshape(-1)
        for table, key_piece, shift in ((with_prev, 2 * piece, BLK), (first, piece, 0)):
            k_pos = (groups * np.arange(key_piece)[None, :]
                     + np.arange(groups)[:, None]).reshape(-1) - shift
            steps = q_pos[:, None] - k_pos[None, :]
            valid = (steps >= 0) & (steps <= N_STEPS)
            bias = -slopes[:, None, None] * (steps * dil)[None] * LOG2E
            table.append(np.where(valid[None], bias, -np.inf).astype(np.float32))
    return np.stack(with_prev), np.stack(first)


def _attn_kernel(q_ref, k_ref, v_ref, bias_prev_ref, bias_first_ref, o_ref,
                 acc_ref, max_ref, s_ref, p_ref):
    for pair in range(PAIRS_PER_STEP):
        lanes = pl.ds(pair * LANES, LANES)
        heads = pl.ds(2 * pair, 2)
        _attend_pair(q_ref.at[:, lanes], k_ref.at[:, lanes], v_ref.at[:, lanes],
                     bias_prev_ref.at[:, heads], bias_first_ref.at[:, heads],
                     o_ref.at[:, lanes], acc_ref, max_ref, s_ref, p_ref)


def _attend_pair(q_ref, k_ref, v_ref, bias_prev_ref, bias_first_ref, o_ref,
                 acc_ref, max_ref, s_ref, p_ref):
    lane = lax.broadcasted_iota(jnp.int32, (1, LANES), 1)
    head_mask = (lane < HEAD_DIM, lane >= HEAD_DIM)

    def pieces(dil, cls, n, with_prev):
        groups = N_CLASSES // dil
        piece = BLK // groups
        first = cls * BLK + (n - 1 if with_prev else n) * piece
        rows = 2 * piece if with_prev else piece
        if not isinstance(first, int):
            first = pl.multiple_of(first, SUBLANES)
        return [((dil * g) * BLK + first, rows) for g in range(groups)]

    def gather(ref, where):
        parts = [ref[pl.ds(start, rows), :] for start, rows in where]
        return parts[0] if len(parts) == 1 else jnp.concatenate(parts, axis=0)

    def scatter(ref, slab, where, value):
        offset = 0
        for start, rows in where:
            ref[slab, pl.ds(start, rows), :] = value[offset:offset + rows]
            offset += rows

    def scores(slot, bi, dil, blocks):
        for u, (cls, n, has_prev) in enumerate(blocks):
            keys = 2 * BLK if has_prev else BLK
            q = gather(q_ref, pieces(dil, cls, n, False)) * (HEAD_DIM ** -0.5 * LOG2E)
            kb = gather(k_ref, pieces(dil, cls, n, has_prev)).astype(BF16)
            for h in range(2):
                qh = jnp.where(head_mask[h], q, 0.0).astype(BF16)
                s = lax.dot_general(qh, kb, (((1,), (1,)), ((), ())),
                                    preferred_element_type=F32)
                bias = bias_prev_ref[bi, h] if has_prev else bias_first_ref[bi, h]
                s_ref[slot, 2 * u + h, :, :keys] = s + bias

    def softmax(slot, bi, dil, blocks):
        for u, (cls, n, has_prev) in enumerate(blocks):
            keys = 2 * BLK if has_prev else BLK
            for h in range(2):
                s = s_ref[slot, 2 * u + h, :, :keys]
                m = jnp.max(s, axis=1, keepdims=True)
                p_ref[slot, 2 * u + h, :, :keys] = jnp.exp2(s - m).astype(BF16)
                scatter(max_ref, 2 * bi + h, pieces(dil, cls, n, False),
                        jnp.broadcast_to(m, (BLK, LANES)))

    def weighted_values(slot, bi, dil, blocks):
        for u, (cls, n, has_prev) in enumerate(blocks):
            keys = 2 * BLK if has_prev else BLK
            v = gather(v_ref, pieces(dil, cls, n, has_prev))
            for h in range(2):
                v_aug = jnp.where(head_mask[h], v, 1.0).astype(BF16)
                scatter(acc_ref, 2 * bi + h, pieces(dil, cls, n, False),
                        jnp.dot(p_ref[slot, 2 * u + h, :, :keys], v_aug,
                                preferred_element_type=F32))

    groups = []
    for bi, dil in enumerate(DILATIONS):
        blocks = [(cls, n, n > 0) for cls in range(dil) for n in range(N_CLASSES // dil)]
        groups += [(bi, dil, blocks[i:i + ATTN_UNROLL])
                   for i in range(0, len(blocks), ATTN_UNROLL)]
    for i in range(len(groups) + 2):
        if i < len(groups):
            scores(i % 2, *groups[i])
        if 1 <= i <= len(groups):
            softmax((i - 1) % 2, *groups[i - 1])
        if i >= 2:
            weighted_values(i % 2, *groups[i - 2])

    for c in range(o_ref.shape[0] // COMBINE_CHUNK):
        rows = pl.ds(c * COMBINE_CHUNK, COMBINE_CHUNK)
        tot = []
        for h in range(2):
            ms = [max_ref[2 * bi + h, rows, :] for bi in range(len(DILATIONS))]
            m = jnp.maximum(jnp.maximum(ms[0], ms[1]), ms[2])
            tot.append(sum(jnp.exp2(ms[bi] - m) * acc_ref[2 * bi + h, rows, :]
                           for bi in range(len(DILATIONS))))
        num = jnp.where(head_mask[0], tot[0], tot[1])
        den = pltpu.roll(jnp.where(head_mask[0], tot[1], tot[0]), HEAD_DIM, axis=1)
        o_ref[rows, :] = (num / den).astype(o_ref.dtype)


def _attention(qkv, bias_prev, bias_first, batch, seq):
    steps = N_HEADS // 2 // PAIRS_PER_STEP
    width = PAIRS_PER_STEP * LANES
    n_slabs = 2 * len(DILATIONS)
    n_dil = len(DILATIONS)
    return pl.pallas_call(
        _attn_kernel,
        grid=(batch, steps),
        in_specs=[pl.BlockSpec((seq, width), lambda b, p: (b, p)),
                  pl.BlockSpec((seq, width), lambda b, p: (b, steps + p)),
                  pl.BlockSpec((seq, width), lambda b, p: (b, 2 * steps + p)),
                  pl.BlockSpec((n_dil, 2 * PAIRS_PER_STEP, BLK, 2 * BLK),
                               lambda b, p: (0, p, 0, 0)),
                  pl.BlockSpec((n_dil, 2 * PAIRS_PER_STEP, BLK, BLK),
                               lambda b, p: (0, p, 0, 0))],
        out_specs=pl.BlockSpec((seq, width), lambda b, p: (b, p)),
        out_shape=jax.ShapeDtypeStruct((batch * seq, ATTN_WIDTH), BF16),
        scratch_shapes=[pltpu.VMEM((n_slabs, seq, LANES), F32),
                        pltpu.VMEM((n_slabs, seq, LANES), F32),
                        pltpu.VMEM((2, 2 * ATTN_UNROLL, BLK, 2 * BLK), F32),
                        pltpu.VMEM((2, 2 * ATTN_UNROLL, BLK, 2 * BLK), BF16)],
        compiler_params=_params(2), name="dilated_attn")(qkv, qkv, qkv, bias_prev, bias_first)


def _conv_kernel(z_ref, gw_ref, cw_ref, cb_ref, lg_ref, lb_ref, o_ref, u_ref, pre_ref):
    cls_rows = z_ref.shape[0] // N_CLASSES
    g0, g1, g2 = 0, GCONV_WIDTH, 2 * GCONV_WIDTH
    c0 = 3 * GCONV_WIDTH
    c1 = c0 + CCONV_WIDTH
    row = lax.broadcasted_iota(jnp.int32, (cls_rows, 1), 0)

    def fill(conv_input, n_shifts):
        def body(r, carry):
            rows = pl.ds(pl.multiple_of(r * cls_rows, cls_rows), cls_rows)
            u = conv_input(rows)
            u_ref[0, rows, :] = u
            for d in range(1, n_shifts):
                u_ref[d, rows, :] = jnp.where(row >= d, pltpu.roll(u, d, axis=0), 0.0)
            return carry
        lax.fori_loop(0, N_CLASSES, body, 0)

    def conv_tile(w_ref, n_taps, r, row0):
        acc = None
        for s in range(n_taps):
            whole, part = divmod(s, N_CLASSES)
            borrow = jnp.where(r < part, 1, 0) if part else 0
            src_cls = r - part + N_CLASSES * borrow
            start = pl.multiple_of(src_cls * cls_rows + row0, CONV_ROWS)
            src = u_ref[whole + borrow, pl.ds(start, CONV_ROWS), :]
            term = w_ref[n_taps - 1 - s:n_taps - s, :] * src
            acc = term if acc is None else acc + term
        return acc

    def tiles(body):
        def step(i, carry):
            r = i // (cls_rows // CONV_ROWS)
            row0 = (i % (cls_rows // CONV_ROWS)) * CONV_ROWS
            body(r, row0, pl.ds(pl.multiple_of(r * cls_rows + row0, CONV_ROWS), CONV_ROWS))
            return carry
        lax.fori_loop(0, N_CLASSES * (cls_rows // CONV_ROWS), step, 0)

    fill(lambda rows: z_ref[rows, g1:g2] * z_ref[rows, g2:c0],
         (GCONV_K - 1) // N_CLASSES + 2)

    def gconv(r, row0, rows):
        acc = conv_tile(gw_ref, GCONV_K, r, row0)
        o_ref[rows, 0:GCONV_WIDTH] = (z_ref[rows, g0:g1] * acc).astype(o_ref.dtype)

    tiles(gconv)

    fill(lambda rows: z_ref[rows, c0:c1] * jax.nn.sigmoid(z_ref[rows, c1:]), CONV_SHIFTS)

    def conv_class(r):
        for row0 in range(0, cls_rows, CONV_ROWS):
            rows = pl.ds(pl.multiple_of(r * cls_rows + row0, CONV_ROWS), CONV_ROWS)
            pre_ref[rows, :] = conv_tile(cw_ref, CCONV_K, r, row0) + cb_ref[...]

    def norm_class(r):
        rows = pl.ds(pl.multiple_of(r * cls_rows, cls_rows), cls_rows)
        u = pre_ref[rows, :]
        mu = jnp.mean(u, axis=-1, keepdims=True)
        d = u - mu
        var = jnp.mean(d * d, axis=-1, keepdims=True)
        y = d * lax.rsqrt(var + LN_EPS) * lg_ref[...] + lb_ref[...]
        o_ref[rows, GCONV_WIDTH:] = (y * jax.nn.sigmoid(y)).astype(o_ref.dtype)

    conv_class(0)

    def step(r, carry):
        norm_class(r - 1)
        conv_class(r)
        return carry

    lax.fori_loop(1, N_CLASSES, step, 0)
    norm_class(N_CLASSES - 1)


def _conv_mixers(rest, gconv_w, cconv_w, cconv_b, cln_g, cln_b, batch, seq):
    width = GCONV_WIDTH + CCONV_WIDTH
    return pl.pallas_call(
        _conv_kernel,
        grid=(batch,),
        in_specs=[pl.BlockSpec((seq, REST_WIDTH), lambda b: (b, 0)),
                  _resident((GCONV_K, GCONV_WIDTH)), _resident((CCONV_K, CCONV_WIDTH)),
                  _resident((1, CCONV_WIDTH)), _resident((1, CCONV_WIDTH)),
                  _resident((1, CCONV_WIDTH))],
        out_specs=pl.BlockSpec((seq, width), lambda b: (b, 0)),
        out_shape=jax.ShapeDtypeStruct((batch * seq, width), BF16),
        scratch_shapes=[pltpu.VMEM((CONV_SHIFTS, seq, GCONV_WIDTH), F32),
                        pltpu.VMEM((seq, CCONV_WIDTH), F32)],
        compiler_params=_params(1), name="conv_mixers")(
            rest, gconv_w, cconv_w, cconv_b.reshape(1, -1), cln_g.reshape(1, -1),
            cln_b.reshape(1, -1))


def kernel(x, w_in, w_out, gconv_w, cconv_w, cconv_b, cln_g, cln_b, ffn1_wg, ffn1_wu, ffn1_wd, ffn2_wg, ffn2_wu, ffn2_wd, norm_ffn1, norm_mix, norm_ffn2, norm_final):
    batch, seq, _ = x.shape
    depth = w_in.shape[0]
    assert seq == N_CLASSES * BLK and x.shape[2] == D_MODEL
    bias_prev, bias_first = (jnp.asarray(t) for t in _alibi_bias_tables())
    ffn1 = [w.astype(BF16) for w in (ffn1_wg, ffn1_wu, ffn1_wd)]
    ffn2 = [w.astype(BF16) for w in (ffn2_wg, ffn2_wu, ffn2_wd)]
    w_in, w_out = w_in.astype(BF16), w_out.astype(BF16)
    h = x
    for l in range(depth):
        last = l == depth - 1
        h = _ffn(h, l, norm_ffn1[l], *ffn1, natural_in=(batch, seq) if l == 0 else None)
        qkv, rest = _inproj(h, l, norm_mix[l], w_in)
        y_attn = _attention(qkv, bias_prev, bias_first, batch, seq)
        y_conv = _conv_mixers(rest, gconv_w[l], cconv_w[l], cconv_b[l], cln_g[l],
                              cln_b[l], batch, seq)
        h = _ffn(h, l, norm_ffn2[l], *ffn2, final_gain=norm_final if last else None,
                 natural_out=(batch, seq) if last else None,
                 mixed=(y_attn, y_conv, w_out))
    return h
```

```python
import functools
import math

import numpy as np
import jax
import jax.numpy as jnp
from jax import lax
from jax.experimental import pallas as pl
from jax.experimental.pallas import tpu as pltpu

D_MODEL = 1024
D_FF = 11 * D_MODEL // 4
HEAD_DIM = 64
ATTN_WIDTH = D_MODEL // 2
N_HEADS = ATTN_WIDTH // HEAD_DIM
GCONV_WIDTH = D_MODEL // 4
CCONV_WIDTH = D_MODEL // 4
GCONV_K = 3
CCONV_K = 31
QKV_WIDTH = 3 * ATTN_WIDTH
REST_WIDTH = 3 * GCONV_WIDTH + 2 * CCONV_WIDTH
D_IN = QKV_WIDTH + REST_WIDTH
CONV_IN_WIDTH = 2 * GCONV_WIDTH + CCONV_WIDTH
DILATIONS = (1, 4, 16)
N_STEPS = 128
BLK = 128
N_CLASSES = max(DILATIONS)
LANES = 128
SUBLANES = 8
NATURAL_CLASSES = SUBLANES
FFN_ROWS = 256
PROJ_ROWS = 512
FFN_BLOCK = NATURAL_CLASSES * BLK
RMS_EPS = 1e-6
LN_EPS = 1e-5
LOG2E = math.log2(math.e)
ATTN_UNROLL = 2
PAIRS_PER_STEP = 2
COMBINE_CHUNK = 256
CONV_ROWS = 64
CONV_SHIFTS = (CCONV_K - 1) // N_CLASSES + 2
VMEM_LIMIT = 56 * 1024 * 1024

F32 = jnp.float32
BF16 = jnp.bfloat16


def _rms(x, g):
    return x * lax.rsqrt(jnp.mean(x * x, axis=-1, keepdims=True) + RMS_EPS) * g


def _params(n_axes):
    return pltpu.CompilerParams(
        dimension_semantics=("arbitrary",) * n_axes, vmem_limit_bytes=VMEM_LIMIT)


def _resident(shape, layer=None):
    if layer is None:
        return pl.BlockSpec(shape, lambda *_: (0,) * len(shape),
                            pipeline_mode=pl.Buffered(1))
    return pl.BlockSpec((None,) + shape, lambda *_: (layer,) + (0,) * len(shape),
                        pipeline_mode=pl.Buffered(1))


def _ffn_kernel(x_ref, g_ref, wg_ref, wu_ref, wd_ref, *rest, final, natural_in,
                natural_out, project):
    o_ref = rest[-1]
    cls_per_sub = FFN_ROWS // BLK

    def prepare(i):
        rows = pl.ds(i * FFN_ROWS, FFN_ROWS)
        if natural_in:
            x = jnp.concatenate(
                [x_ref[:, i * cls_per_sub + c, :] for c in range(cls_per_sub)], axis=0)
        else:
            x = x_ref[rows, :]
        if project:
            ya_ref, yc_ref, wo_ref = rest[:3]
            x = (x + jnp.dot(ya_ref[rows, :], wo_ref[:ATTN_WIDTH, :],
                             preferred_element_type=F32)
                 + jnp.dot(yc_ref[rows, :], wo_ref[ATTN_WIDTH:, :],
                           preferred_element_type=F32))
        return x, _rms(x, g_ref[...]).astype(BF16)

    def finish(i, x, h):
        gate = jnp.dot(h, wg_ref[...], preferred_element_type=F32)
        up = jnp.dot(h, wu_ref[...], preferred_element_type=F32)
        act = (gate * jax.nn.sigmoid(gate) * up).astype(BF16)
        y = x + 0.5 * jnp.dot(act, wd_ref[...], preferred_element_type=F32)
        if final:
            y = _rms(y, rest[-2][...])
        if natural_out:
            for c in range(cls_per_sub):
                o_ref[:, i * cls_per_sub + c, :] = y[c * BLK:(c + 1) * BLK]
        else:
            o_ref[pl.ds(i * FFN_ROWS, FFN_ROWS), :] = y

    n_sub = (x_ref.shape[1] * BLK if natural_in else x_ref.shape[0]) // FFN_ROWS
    prepared = [prepare(i) for i in range(n_sub)]
    for i in range(n_sub):
        finish(i, *prepared[i])


def _ffn(x, layer, gain, wg, wu, wd, final_gain=None, natural_in=None, natural_out=None,
         mixed=None):
    natural = natural_in or natural_out
    if natural:
        batch, seq = natural
        assert seq == N_CLASSES * BLK
        t = batch * seq
        halves = N_CLASSES // NATURAL_CLASSES
        nat_block = pl.BlockSpec((None, BLK, NATURAL_CLASSES, D_MODEL),
                                 lambda b, j: (b, 0, j, 0))
        cls_block = pl.BlockSpec((FFN_BLOCK, D_MODEL),
                                 lambda b, j: (b * halves + j, 0))
        grid = (batch, halves)
        x_spec = nat_block if natural_in else cls_block
        out_spec = nat_block if natural_out else cls_block
        y_spec = pl.BlockSpec((FFN_BLOCK, ATTN_WIDTH),
                              lambda b, j: (b * halves + j, 0))
        if natural_in:
            x = x.reshape(batch, BLK, N_CLASSES, D_MODEL)
    else:
        t = x.shape[0]
        grid = (t // FFN_BLOCK,)
        x_spec = out_spec = pl.BlockSpec((FFN_BLOCK, D_MODEL), lambda i: (i, 0))
        y_spec = pl.BlockSpec((FFN_BLOCK, ATTN_WIDTH), lambda i: (i, 0))
    out_shape = ((batch, BLK, N_CLASSES, D_MODEL) if natural_out else (t, D_MODEL))
    in_specs = [x_spec, _resident((1, D_MODEL)), _resident((D_MODEL, D_FF), layer),
                _resident((D_MODEL, D_FF), layer), _resident((D_FF, D_MODEL), layer)]
    args = [x, gain.reshape(1, D_MODEL), wg, wu, wd]
    if mixed is not None:
        in_specs += [y_spec, y_spec, _resident((D_MODEL, D_MODEL), layer)]
        args += list(mixed)
    if final_gain is not None:
        in_specs.append(_resident((1, D_MODEL)))
        args.append(final_gain.reshape(1, D_MODEL))
    out = pl.pallas_call(
        functools.partial(_ffn_kernel, final=final_gain is not None,
                          natural_in=bool(natural_in), natural_out=bool(natural_out),
                          project=mixed is not None),
        grid=grid, in_specs=in_specs, out_specs=out_spec,
        out_shape=jax.ShapeDtypeStruct(out_shape, F32),
        compiler_params=_params(len(grid)), name="ffn")(*args)
    return out.reshape(batch, seq, D_MODEL) if natural_out else out


def _inproj_kernel(x_ref, g_ref, w_ref, qkv_ref, conv_ref):
    g1, g2 = GCONV_WIDTH, 2 * GCONV_WIDTH
    c0 = 3 * GCONV_WIDTH
    c1 = c0 + CCONV_WIDTH
    n_sub = x_ref.shape[0] // PROJ_ROWS
    hs = [_rms(x_ref[i * PROJ_ROWS:(i + 1) * PROJ_ROWS, :], g_ref[...]).astype(BF16)
          for i in range(n_sub)]
    for i, h in enumerate(hs):
        rows = slice(i * PROJ_ROWS, (i + 1) * PROJ_ROWS)
        qkv_ref[rows, :] = jnp.dot(h, w_ref[:, :QKV_WIDTH], preferred_element_type=F32)
        z = jnp.dot(h, w_ref[:, QKV_WIDTH:], preferred_element_type=F32)
        conv_ref[rows, :g1] = z[:, :g1]
        conv_ref[rows, g1:g2] = z[:, g1:g2] * z[:, g2:c0]
        conv_ref[rows, g2:] = z[:, c0:c1] * jax.nn.sigmoid(z[:, c1:])


def _inproj(x, layer, gain, w_in, tm=1024):
    t = x.shape[0]
    return pl.pallas_call(
        _inproj_kernel, grid=(t // tm,),
        in_specs=[pl.BlockSpec((tm, D_MODEL), lambda i: (i, 0)),
                  _resident((1, D_MODEL)), _resident((D_MODEL, D_IN), layer)],
        out_specs=[pl.BlockSpec((tm, QKV_WIDTH), lambda i: (i, 0)),
                   pl.BlockSpec((tm, CONV_IN_WIDTH), lambda i: (i, 0))],
        out_shape=[jax.ShapeDtypeStruct((t, QKV_WIDTH), F32),
                   jax.ShapeDtypeStruct((t, CONV_IN_WIDTH), F32)],
        compiler_params=_params(1), name="inproj")(x, gain.reshape(1, D_MODEL), w_in)


def _alibi_bias_tables():
    slopes = np.exp2(-8.0 * np.arange(1, N_HEADS + 1, dtype=np.float64) / N_HEADS)
    with_prev, first = [], []
    for dil in DILATIONS:
        groups = N_CLASSES // dil
        piece = BLK // groups
        q_pos = (groups * np.arange(piece)[None, :] + np.arange(groups)[:, None]).reshape(-1)
        for table, key_piece, shift in ((with_prev, 2 * piece, BLK), (first, piece, 0)):
            k_pos = (groups * np.arange(key_piece)[None, :]
                     + np.arange(groups)[:, None]).reshape(-1) - shift
            steps = q_pos[:, None] - k_pos[None, :]
            valid = (steps >= 0) & (steps <= N_STEPS)
            bias = -slopes[:, None, None] * (steps * dil)[None] * LOG2E
            table.append(np.where(valid[None], bias, -np.inf).astype(np.float32))
    return np.stack(with_prev), np.stack(first)


def _attn_kernel(q_ref, k_ref, v_ref, bias_prev_ref, bias_first_ref, o_ref,
                 acc_ref, max_ref, s_ref, p_ref):
    for pair in range(PAIRS_PER_STEP):
        lanes = pl.ds(pair * LANES, LANES)
        heads = pl.ds(2 * pair, 2)
        _attend_pair(q_ref.at[:, lanes], k_ref.at[:, lanes], v_ref.at[:, lanes],
                     bias_prev_ref.at[:, heads], bias_first_ref.at[:, heads],
                     o_ref.at[:, lanes], acc_ref, max_ref, s_ref, p_ref)


def _attend_pair(q_ref, k_ref, v_ref, bias_prev_ref, bias_first_ref, o_ref,
                 acc_ref, max_ref, s_ref, p_ref):
    lane = lax.broadcasted_iota(jnp.int32, (1, LANES), 1)
    head_mask = (lane < HEAD_DIM, lane >= HEAD_DIM)

    def pieces(dil, cls, n, with_prev):
        groups = N_CLASSES // dil
        piece = BLK // groups
        first = cls * BLK + (n - 1 if with_prev else n) * piece
        rows = 2 * piece if with_prev else piece
        if not isinstance(first, int):
            first = pl.multiple_of(first, SUBLANES)
        return [((dil * g) * BLK + first, rows) for g in range(groups)]

    def gather(ref, where):
        parts = [ref[pl.ds(start, rows), :] for start, rows in where]
        return parts[0] if len(parts) == 1 else jnp.concatenate(parts, axis=0)

    def scatter(ref, slab, where, value):
        offset = 0
        for start, rows in where:
            ref[slab, pl.ds(start, rows), :] = value[offset:offset + rows]
            offset += rows

    def scores(slot, bi, dil, blocks):
        for u, (cls, n, has_prev) in enumerate(blocks):
            keys = 2 * BLK if has_prev else BLK
            q = gather(q_ref, pieces(dil, cls, n, False)) * (HEAD_DIM ** -0.5 * LOG2E)
            kb = gather(k_ref, pieces(dil, cls, n, has_prev)).astype(BF16)
            for h in range(2):
                qh = jnp.where(head_mask[h], q, 0.0).astype(BF16)
                s = lax.dot_general(qh, kb, (((1,), (1,)), ((), ())),
                                    preferred_element_type=F32)
                bias = bias_prev_ref[bi, h] if has_prev else bias_first_ref[bi, h]
                s_ref[slot, 2 * u + h, :, :keys] = s + bias

    def softmax(slot, bi, dil, blocks):
        for u, (cls, n, has_prev) in enumerate(blocks):
            keys = 2 * BLK if has_prev else BLK
            for h in range(2):
                s = s_ref[slot, 2 * u + h, :, :keys]
                m = jnp.max(s, axis=1, keepdims=True)
                p_ref[slot, 2 * u + h, :, :keys] = jnp.exp2(s - m).astype(BF16)
                scatter(max_ref, 2 * bi + h, pieces(dil, cls, n, False),
                        jnp.broadcast_to(m, (BLK, LANES)))

    def weighted_values(slot, bi, dil, blocks):
        for u, (cls, n, has_prev) in enumerate(blocks):
            keys = 2 * BLK if has_prev else BLK
            v = gather(v_ref, pieces(dil, cls, n, has_prev))
            for h in range(2):
                v_aug = jnp.where(head_mask[h], v, 1.0).astype(BF16)
                scatter(acc_ref, 2 * bi + h, pieces(dil, cls, n, False),
                        jnp.dot(p_ref[slot, 2 * u + h, :, :keys], v_aug,
                                preferred_element_type=F32))

    groups = []
    for bi, dil in enumerate(DILATIONS):
        blocks = [(cls, n, n > 0) for cls in range(dil) for n in range(N_CLASSES // dil)]
        groups += [(bi, dil, blocks[i:i + ATTN_UNROLL])
                   for i in range(0, len(blocks), ATTN_UNROLL)]
    for i in range(len(groups) + 2):
        if i < len(groups):
            scores(i % 2, *groups[i])
        if 1 <= i <= len(groups):
            softmax((i - 1) % 2, *groups[i - 1])
        if i >= 2:
            weighted_values(i % 2, *groups[i - 2])

    for c in range(o_ref.shape[0] // COMBINE_CHUNK):
        rows = pl.ds(c * COMBINE_CHUNK, COMBINE_CHUNK)
        tot = []
        for h in range(2):
            ms = [max_ref[2 * bi + h, rows, :] for bi in range(len(DILATIONS))]
            m = jnp.maximum(jnp.maximum(ms[0], ms[1]), ms[2])
            tot.append(sum(jnp.exp2(ms[bi] - m) * acc_ref[2 * bi + h, rows, :]
                           for bi in range(len(DILATIONS))))
        num = jnp.where(head_mask[0], tot[0], tot[1])
        den = pltpu.roll(jnp.where(head_mask[0], tot[1], tot[0]), HEAD_DIM, axis=1)
        o_ref[rows, :] = (num / den).astype(o_ref.dtype)


def _attention(qkv, bias_prev, bias_first, batch, seq):
    steps = N_HEADS // 2 // PAIRS_PER_STEP
    width = PAIRS_PER_STEP * LANES
    n_slabs = 2 * len(DILATIONS)
    n_dil = len(DILATIONS)
    return pl.pallas_call(
        _attn_kernel,
        grid=(batch, steps),
        in_specs=[pl.BlockSpec((seq, width), lambda b, p: (b, p)),
                  pl.BlockSpec((seq, width), lambda b, p: (b, steps + p)),
                  pl.BlockSpec((seq, width), lambda b, p: (b, 2 * steps + p)),
                  pl.BlockSpec((n_dil, 2 * PAIRS_PER_STEP, BLK, 2 * BLK),
                               lambda b, p: (0, p, 0, 0)),
                  pl.BlockSpec((n_dil, 2 * PAIRS_PER_STEP, BLK, BLK),
                               lambda b, p: (0, p, 0, 0))],
        out_specs=pl.BlockSpec((seq, width), lambda b, p: (b, p)),
        out_shape=jax.ShapeDtypeStruct((batch * seq, ATTN_WIDTH), BF16),
        scratch_shapes=[pltpu.VMEM((n_slabs, seq, LANES), F32),
                        pltpu.VMEM((n_slabs, seq, LANES), F32),
                        pltpu.VMEM((2, 2 * ATTN_UNROLL, BLK, 2 * BLK), F32),
                        pltpu.VMEM((2, 2 * ATTN_UNROLL, BLK, 2 * BLK), BF16)],
        compiler_params=_params(2), name="dilated_attn")(qkv, qkv, qkv, bias_prev, bias_first)


def _conv_kernel(z_ref, gw_ref, cw_ref, cb_ref, lg_ref, lb_ref, o_ref, u_ref, pre_ref):
    cls_rows = z_ref.shape[0] // N_CLASSES
    g1, g2 = GCONV_WIDTH, 2 * GCONV_WIDTH
    row = lax.broadcasted_iota(jnp.int32, (cls_rows, 1), 0)

    def fill(conv_input, n_shifts):
        def body(r, carry):
            rows = pl.ds(pl.multiple_of(r * cls_rows, cls_rows), cls_rows)
            u = conv_input(rows)
            u_ref[0, rows, :] = u
            for d in range(1, n_shifts):
                u_ref[d, rows, :] = jnp.where(row >= d, pltpu.roll(u, d, axis=0), 0.0)
            return carry
        lax.fori_loop(0, N_CLASSES, body, 0)

    def conv_tile(w_ref, n_taps, r, row0):
        acc = None
        for s in range(n_taps):
            whole, part = divmod(s, N_CLASSES)
            borrow = jnp.where(r < part, 1, 0) if part else 0
            src_cls = r - part + N_CLASSES * borrow
            start = pl.multiple_of(src_cls * cls_rows + row0, CONV_ROWS)
            src = u_ref[whole + borrow, pl.ds(start, CONV_ROWS), :]
            term = w_ref[n_taps - 1 - s:n_taps - s, :] * src
            acc = term if acc is None else acc + term
        return acc

    def tiles(body):
        def step(i, carry):
            r = i // (cls_rows // CONV_ROWS)
            row0 = (i % (cls_rows // CONV_ROWS)) * CONV_ROWS
            body(r, row0, pl.ds(pl.multiple_of(r * cls_rows + row0, CONV_ROWS), CONV_ROWS))
            return carry
        lax.fori_loop(0, N_CLASSES * (cls_rows // CONV_ROWS), step, 0)

    fill(lambda rows: z_ref[rows, g1:g2],
         (GCONV_K - 1) // N_CLASSES + 2)

    def gconv(r, row0, rows):
        acc = conv_tile(gw_ref, GCONV_K, r, row0)
        o_ref[rows, 0:GCONV_WIDTH] = (z_ref[rows, :g1] * acc).astype(o_ref.dtype)

    tiles(gconv)

    fill(lambda rows: z_ref[rows, g2:], CONV_SHIFTS)

    def conv_class(r):
        for row0 in range(0, cls_rows, CONV_ROWS):
            rows = pl.ds(pl.multiple_of(r * cls_rows + row0, CONV_ROWS), CONV_ROWS)
            pre_ref[rows, :] = conv_tile(cw_ref, CCONV_K, r, row0) + cb_ref[...]

    def norm_class(r):
        rows = pl.ds(pl.multiple_of(r * cls_rows, cls_rows), cls_rows)
        u = pre_ref[rows, :]
        mu = jnp.mean(u, axis=-1, keepdims=True)
        d = u - mu
        var = jnp.mean(d * d, axis=-1, keepdims=True)
        y = d * lax.rsqrt(var + LN_EPS) * lg_ref[...] + lb_ref[...]
        o_ref[rows, GCONV_WIDTH:] = (y * jax.nn.sigmoid(y)).astype(o_ref.dtype)

    conv_class(0)

    def step(r, carry):
        norm_class(r - 1)
        conv_class(r)
        return carry

    lax.fori_loop(1, N_CLASSES, step, 0)
    norm_class(N_CLASSES - 1)


def _conv_mixers(rest, gconv_w, cconv_w, cconv_b, cln_g, cln_b, batch, seq):
    width = GCONV_WIDTH + CCONV_WIDTH
    return pl.pallas_call(
        _conv_kernel,
        grid=(batch,),
        in_specs=[pl.BlockSpec((seq, CONV_IN_WIDTH), lambda b: (b, 0)),
                  _resident((GCONV_K, GCONV_WIDTH)), _resident((CCONV_K, CCONV_WIDTH)),
                  _resident((1, CCONV_WIDTH)), _resident((1, CCONV_WIDTH)),
                  _resident((1, CCONV_WIDTH))],
        out_specs=pl.BlockSpec((seq, width), lambda b: (b, 0)),
        out_shape=jax.ShapeDtypeStruct((batch * seq, width), BF16),
        scratch_shapes=[pltpu.VMEM((CONV_SHIFTS, seq, GCONV_WIDTH), F32),
                        pltpu.VMEM((seq, CCONV_WIDTH), F32)],
        compiler_params=_params(1), name="conv_mixers")(
            rest, gconv_w, cconv_w, cconv_b.reshape(1, -1), cln_g.reshape(1, -1),
            cln_b.reshape(1, -1))


def kernel(x, w_in, w_out, gconv_w, cconv_w, cconv_b, cln_g, cln_b, ffn1_wg, ffn1_wu, ffn1_wd, ffn2_wg, ffn2_wu, ffn2_wd, norm_ffn1, norm_mix, norm_ffn2, norm_final):
    batch, seq, _ = x.shape
    depth = w_in.shape[0]
    assert seq == N_CLASSES * BLK and x.shape[2] == D_MODEL
    bias_prev, bias_first = (jnp.asarray(t) for t in _alibi_bias_tables())
    ffn1 = [w.astype(BF16) for w in (ffn1_wg, ffn1_wu, ffn1_wd)]
    ffn2 = [w.astype(BF16) for w in (ffn2_wg, ffn2_wu, ffn2_wd)]
    w_in, w_out = w_in.astype(BF16), w_out.astype(BF16)
    h = x
    for l in range(depth):
        last = l == depth - 1
        h = _ffn(h, l, norm_ffn1[l], *ffn1, natural_in=(batch, seq) if l == 0 else None)
        qkv, rest = _inproj(h, l, norm_mix[l], w_in)
        y_attn = _attention(qkv, bias_prev, bias_first, batch, seq)
        y_conv = _conv_mixers(rest, gconv_w[l], cconv_w[l], cconv_b[l], cln_g[l],
                              cln_b[l], batch, seq)
        h = _ffn(h, l, norm_ffn2[l], *ffn2, final_gain=norm_final if last else None,
                 natural_out=(batch, seq) if last else None,
                 mixed=(y_attn, y_conv, w_out))
    return h
```

```python
import functools
import math

import numpy as np
import jax
import jax.numpy as jnp
from jax import lax
from jax.experimental import pallas as pl
from jax.experimental.pallas import tpu as pltpu

D_MODEL = 1024
D_FF = 11 * D_MODEL // 4
HEAD_DIM = 64
ATTN_WIDTH = D_MODEL // 2
N_HEADS = ATTN_WIDTH // HEAD_DIM
GCONV_WIDTH = D_MODEL // 4
CCONV_WIDTH = D_MODEL // 4
GCONV_K = 3
CCONV_K = 31
QKV_WIDTH = 3 * ATTN_WIDTH
REST_WIDTH = 3 * GCONV_WIDTH + 2 * CCONV_WIDTH
D_IN = QKV_WIDTH + REST_WIDTH
CONV_IN_WIDTH = 2 * GCONV_WIDTH + CCONV_WIDTH
DILATIONS = (1, 4, 16)
N_STEPS = 128
BLK = 128
N_CLASSES = max(DILATIONS)
LANES = 128
SUBLANES = 8
NATURAL_CLASSES = SUBLANES
FFN_ROWS = 256
PROJ_ROWS = 512
FFN_BLOCK = NATURAL_CLASSES * BLK
RMS_EPS = 1e-6
LN_EPS = 1e-5
LOG2E = math.log2(math.e)
ATTN_UNROLL = 2
PAIRS_PER_STEP = 2
COMBINE_CHUNK = 256
CONV_ROWS = 64
CONV_SHIFTS = (CCONV_K - 1) // N_CLASSES + 2
VMEM_LIMIT = 56 * 1024 * 1024

F32 = jnp.float32
BF16 = jnp.bfloat16


def _rms(x, g):
    return x * lax.rsqrt(jnp.mean(x * x, axis=-1, keepdims=True) + RMS_EPS) * g


def _params(n_axes):
    return pltpu.CompilerParams(
        dimension_semantics=("arbitrary",) * n_axes, vmem_limit_bytes=VMEM_LIMIT)


def _resident(shape, layer=None):
    if layer is None:
        return pl.BlockSpec(shape, lambda *_: (0,) * len(shape),
                            pipeline_mode=pl.Buffered(1))
    return pl.BlockSpec((None,) + shape, lambda *_: (layer,) + (0,) * len(shape),
                        pipeline_mode=pl.Buffered(1))


def _ffn_kernel(x_ref, g_ref, wg_ref, wu_ref, wd_ref, *rest, final, natural_in,
                natural_out, project):
    o_ref = rest[-1]
    cls_per_sub = FFN_ROWS // BLK

    def prepare(i):
        rows = pl.ds(i * FFN_ROWS, FFN_ROWS)
        if natural_in:
            x = jnp.concatenate(
                [x_ref[:, i * cls_per_sub + c, :] for c in range(cls_per_sub)], axis=0)
        else:
            x = x_ref[rows, :]
        if project:
            ya_ref, yc_ref, wo_ref = rest[:3]
            x = (x + jnp.dot(ya_ref[rows, :], wo_ref[:ATTN_WIDTH, :],
                             preferred_element_type=F32)
                 + jnp.dot(yc_ref[rows, :], wo_ref[ATTN_WIDTH:, :],
                           preferred_element_type=F32))
        return x, _rms(x, g_ref[...]).astype(BF16)

    def finish(i, x, h):
        gate = jnp.dot(h, wg_ref[...], preferred_element_type=F32)
        up = jnp.dot(h, wu_ref[...], preferred_element_type=F32)
        act = (gate * jax.nn.sigmoid(gate) * up).astype(BF16)
        y = x + 0.5 * jnp.dot(act, wd_ref[...], preferred_element_type=F32)
        if final:
            y = _rms(y, rest[-2][...])
        if natural_out:
            for c in range(cls_per_sub):
                o_ref[:, i * cls_per_sub + c, :] = y[c * BLK:(c + 1) * BLK]
        else:
            o_ref[pl.ds(i * FFN_ROWS, FFN_ROWS), :] = y

    n_sub = (x_ref.shape[1] * BLK if natural_in else x_ref.shape[0]) // FFN_ROWS
    prepared = [prepare(i) for i in range(n_sub)]
    for i in range(n_sub):
        finish(i, *prepared[i])


def _ffn(x, layer, gain, wg, wu, wd, final_gain=None, natural_in=None, natural_out=None,
         mixed=None):
    natural = natural_in or natural_out
    if natural:
        batch, seq = natural
        assert seq == N_CLASSES * BLK
        t = batch * seq
        halves = N_CLASSES // NATURAL_CLASSES
        nat_block = pl.BlockSpec((None, BLK, NATURAL_CLASSES, D_MODEL),
                                 lambda b, j: (b, 0, j, 0))
        cls_block = pl.BlockSpec((FFN_BLOCK, D_MODEL),
                                 lambda b, j: (b * halves + j, 0))
        grid = (batch, halves)
        x_spec = nat_block if natural_in else cls_block
        out_spec = nat_block if natural_out else cls_block
        y_spec = pl.BlockSpec((FFN_BLOCK, ATTN_WIDTH),
                              lambda b, j: (b * halves + j, 0))
        if natural_in:
            x = x.reshape(batch, BLK, N_CLASSES, D_MODEL)
    else:
        t = x.shape[0]
        grid = (t // FFN_BLOCK,)
        x_spec = out_spec = pl.BlockSpec((FFN_BLOCK, D_MODEL), lambda i: (i, 0))
        y_spec = pl.BlockSpec((FFN_BLOCK, ATTN_WIDTH), lambda i: (i, 0))
    out_shape = ((batch, BLK, N_CLASSES, D_MODEL) if natural_out else (t, D_MODEL))
    in_specs = [x_spec, _resident((1, D_MODEL)), _resident((D_MODEL, D_FF), layer),
                _resident((D_MODEL, D_FF), layer), _resident((D_FF, D_MODEL), layer)]
    args = [x, gain.reshape(1, D_MODEL), wg, wu, wd]
    if mixed is not None:
        in_specs += [y_spec, y_spec, _resident((D_MODEL, D_MODEL), layer)]
        args += list(mixed)
    if final_gain is not None:
        in_specs.append(_resident((1, D_MODEL)))
        args.append(final_gain.reshape(1, D_MODEL))
    out = pl.pallas_call(
        functools.partial(_ffn_kernel, final=final_gain is not None,
                          natural_in=bool(natural_in), natural_out=bool(natural_out),
                          project=mixed is not None),
        grid=grid, in_specs=in_specs, out_specs=out_spec,
        out_shape=jax.ShapeDtypeStruct(out_shape, F32),
        compiler_params=_params(len(grid)), name="ffn")(*args)
    return out.reshape(batch, seq, D_MODEL) if natural_out else out


def _inproj_kernel(x_ref, g_ref, w_ref, qkv_ref, conv_ref):
    g1, g2 = GCONV_WIDTH, 2 * GCONV_WIDTH
    c0 = 3 * GCONV_WIDTH
    c1 = c0 + CCONV_WIDTH
    n_sub = x_ref.shape[0] // PROJ_ROWS
    hs = [_rms(x_ref[i * PROJ_ROWS:(i + 1) * PROJ_ROWS, :], g_ref[...]).astype(BF16)
          for i in range(n_sub)]
    for i, h in enumerate(hs):
        rows = slice(i * PROJ_ROWS, (i + 1) * PROJ_ROWS)
        qkv_ref[rows, :] = jnp.dot(h, w_ref[:, :QKV_WIDTH], preferred_element_type=F32)
        z = jnp.dot(h, w_ref[:, QKV_WIDTH:], preferred_element_type=F32)
        conv_ref[rows, :g1] = z[:, :g1]
        conv_ref[rows, g1:g2] = z[:, g1:g2] * z[:, g2:c0]
        conv_ref[rows, g2:] = z[:, c0:c1] * jax.nn.sigmoid(z[:, c1:])


def _inproj(x, layer, gain, w_in, tm=1024):
    t = x.shape[0]
    return pl.pallas_call(
        _inproj_kernel, grid=(t // tm,),
        in_specs=[pl.BlockSpec((tm, D_MODEL), lambda i: (i, 0)),
                  _resident((1, D_MODEL)), _resident((D_MODEL, D_IN), layer)],
        out_specs=[pl.BlockSpec((tm, QKV_WIDTH), lambda i: (i, 0)),
                   pl.BlockSpec((tm, CONV_IN_WIDTH), lambda i: (i, 0))],
        out_shape=[jax.ShapeDtypeStruct((t, QKV_WIDTH), F32),
                   jax.ShapeDtypeStruct((t, CONV_IN_WIDTH), F32)],
        compiler_params=_params(1), name="inproj")(x, gain.reshape(1, D_MODEL), w_in)


def _alibi_bias_tables():
    slopes = np.exp2(-8.0 * np.arange(1, N_HEADS + 1, dtype=np.float64) / N_HEADS)
    with_prev, first = [], []
    for dil in DILATIONS:
        groups = N_CLASSES // dil
        piece = BLK // groups
        q_pos = (groups * np.arange(piece)[None, :] + np.arange(groups)[:, None]).reshape(-1)
        for table, key_piece, shift in ((with_prev, 2 * piece, BLK), (first, piece, 0)):
            k_pos = (groups * np.arange(key_piece)[None, :]
                     + np.arange(groups)[:, None]).reshape(-1) - shift
            steps = q_pos[:, None] - k_pos[None, :]
            valid = (steps >= 0) & (steps <= N_STEPS)
            bias = -slopes[:, None, None] * (steps * dil)[None] * LOG2E
            table.append(np.where(valid[None], bias, -np.inf).astype(np.float32))
    return np.stack(with_prev), np.stack(first)


def _attn_kernel(q_ref, k_ref, v_ref, bias_prev_ref, bias_first_ref, o_ref,
                 acc_ref, max_ref, s_ref, p_ref):
    for pair in range(PAIRS_PER_STEP):
        lanes = pl.ds(pair * LANES, LANES)
        heads = pl.ds(2 * pair, 2)
        _attend_pair(q_ref.at[:, lanes], k_ref.at[:, lanes], v_ref.at[:, lanes],
                     bias_prev_ref.at[:, heads], bias_first_ref.at[:, heads],
                     o_ref.at[:, lanes], acc_ref, max_ref, s_ref, p_ref)


def _attend_pair(q_ref, k_ref, v_ref, bias_prev_ref, bias_first_ref, o_ref,
                 acc_ref, max_ref, s_ref, p_ref):
    lane = lax.broadcasted_iota(jnp.int32, (1, LANES), 1)
    head_mask = (lane < HEAD_DIM, lane >= HEAD_DIM)

    def pieces(dil, cls, n, with_prev):
        groups = N_CLASSES // dil
        piece = BLK // groups
        first = cls * BLK + (n - 1 if with_prev else n) * piece
        rows = 2 * piece if with_prev else piece
        if not isinstance(first, int):
            first = pl.multiple_of(first, SUBLANES)
        return [((dil * g) * BLK + first, rows) for g in range(groups)]

    def gather(ref, where):
        parts = [ref[pl.ds(start, rows), :] for start, rows in where]
        return parts[0] if len(parts) == 1 else jnp.concatenate(parts, axis=0)

    def scatter(ref, slab, where, value):
        offset = 0
        for start, rows in where:
            ref[slab, pl.ds(start, rows), :] = value[offset:offset + rows]
            offset += rows

    def scores(slot, bi, dil, blocks):
        for u, (cls, n, has_prev) in enumerate(blocks):
            keys = 2 * BLK if has_prev else BLK
            q = gather(q_ref, pieces(dil, cls, n, False)) * (HEAD_DIM ** -0.5 * LOG2E)
            kb = gather(k_ref, pieces(dil, cls, n, has_prev)).astype(BF16)
            for h in range(2):
                qh = jnp.where(head_mask[h], q, 0.0).astype(BF16)
                s = lax.dot_general(qh, kb, (((1,), (1,)), ((), ())),
                                    preferred_element_type=F32)
                bias = bias_prev_ref[bi, h] if has_prev else bias_first_ref[bi, h]
                s_ref[slot, 2 * u + h, :, :keys] = s + bias

    def softmax(slot, bi, dil, blocks):
        for u, (cls, n, has_prev) in enumerate(blocks):
            keys = 2 * BLK if has_prev else BLK
            for h in range(2):
                s = s_ref[slot, 2 * u + h, :, :keys]
                m = jnp.max(s, axis=1, keepdims=True)
                p_ref[slot, 2 * u + h, :, :keys] = jnp.exp2(s - m).astype(BF16)
                scatter(max_ref, 2 * bi + h, pieces(dil, cls, n, False),
                        jnp.broadcast_to(m, (BLK, LANES)))

    def weighted_values(slot, bi, dil, blocks):
        for u, (cls, n, has_prev) in enumerate(blocks):
            keys = 2 * BLK if has_prev else BLK
            v = gather(v_ref, pieces(dil, cls, n, has_prev))
            for h in range(2):
                v_aug = jnp.where(head_mask[h], v, 1.0).astype(BF16)
                scatter(acc_ref, 2 * bi + h, pieces(dil, cls, n, False),
                        jnp.dot(p_ref[slot, 2 * u + h, :, :keys], v_aug,
                                preferred_element_type=F32))

    groups = []
    for bi, dil in enumerate(DILATIONS):
        blocks = [(cls, n, n > 0) for cls in range(dil) for n in range(N_CLASSES // dil)]
        size = ATTN_UNROLL * (2 if N_CLASSES // dil == 1 else 1)
        groups += [(bi, dil, blocks[i:i + size]) for i in range(0, len(blocks), size)]
    for i in range(len(groups) + 2):
        if i < len(groups):
            scores(i % 2, *groups[i])
        if 1 <= i <= len(groups):
            softmax((i - 1) % 2, *groups[i - 1])
        if i >= 2:
            weighted_values(i % 2, *groups[i - 2])

    for c in range(o_ref.shape[0] // COMBINE_CHUNK):
        rows = pl.ds(c * COMBINE_CHUNK, COMBINE_CHUNK)
        tot = []
        for h in range(2):
            ms = [max_ref[2 * bi + h, rows, :] for bi in range(len(DILATIONS))]
            m = jnp.maximum(jnp.maximum(ms[0], ms[1]), ms[2])
            tot.append(sum(jnp.exp2(ms[bi] - m) * acc_ref[2 * bi + h, rows, :]
                           for bi in range(len(DILATIONS))))
        num = jnp.where(head_mask[0], tot[0], tot[1])
        den = pltpu.roll(jnp.where(head_mask[0], tot[1], tot[0]), HEAD_DIM, axis=1)
        o_ref[rows, :] = (num / den).astype(o_ref.dtype)


def _attention(qkv, bias_prev, bias_first, batch, seq):
    steps = N_HEADS // 2 // PAIRS_PER_STEP
    width = PAIRS_PER_STEP * LANES
    n_slabs = 2 * len(DILATIONS)
    n_dil = len(DILATIONS)
    return pl.pallas_call(
        _attn_kernel,
        grid=(batch, steps),
        in_specs=[pl.BlockSpec((seq, width), lambda b, p: (b, p)),
                  pl.BlockSpec((seq, width), lambda b, p: (b, steps + p)),
                  pl.BlockSpec((seq, width), lambda b, p: (b, 2 * steps + p)),
                  pl.BlockSpec((n_dil, 2 * PAIRS_PER_STEP, BLK, 2 * BLK),
                               lambda b, p: (0, p, 0, 0)),
                  pl.BlockSpec((n_dil, 2 * PAIRS_PER_STEP, BLK, BLK),
                               lambda b, p: (0, p, 0, 0))],
        out_specs=pl.BlockSpec((seq, width), lambda b, p: (b, p)),
        out_shape=jax.ShapeDtypeStruct((batch * seq, ATTN_WIDTH), BF16),
        scratch_shapes=[pltpu.VMEM((n_slabs, seq, LANES), F32),
                        pltpu.VMEM((n_slabs, seq, LANES), F32),
                        pltpu.VMEM((2, 4 * ATTN_UNROLL, BLK, 2 * BLK), F32),
                        pltpu.VMEM((2, 4 * ATTN_UNROLL, BLK, 2 * BLK), BF16)],
        compiler_params=_params(2), name="dilated_attn")(qkv, qkv, qkv, bias_prev, bias_first)


def _conv_kernel(z_ref, gw_ref, cw_ref, cb_ref, lg_ref, lb_ref, o_ref, u_ref, pre_ref):
    cls_rows = z_ref.shape[0] // N_CLASSES
    g1, g2 = GCONV_WIDTH, 2 * GCONV_WIDTH
    row = lax.broadcasted_iota(jnp.int32, (cls_rows, 1), 0)

    def fill(conv_input, n_shifts):
        def body(r, carry):
            rows = pl.ds(pl.multiple_of(r * cls_rows, cls_rows), cls_rows)
            u = conv_input(rows)
            u_ref[0, rows, :] = u
            for d in range(1, n_shifts):
                u_ref[d, rows, :] = jnp.where(row >= d, pltpu.roll(u, d, axis=0), 0.0)
            return carry
        lax.fori_loop(0, N_CLASSES, body, 0)

    def conv_tile(w_ref, n_taps, r, row0):
        acc = None
        for s in range(n_taps):
            whole, part = divmod(s, N_CLASSES)
            borrow = jnp.where(r < part, 1, 0) if part else 0
            src_cls = r - part + N_CLASSES * borrow
            start = pl.multiple_of(src_cls * cls_rows + row0, CONV_ROWS)
            src = u_ref[whole + borrow, pl.ds(start, CONV_ROWS), :]
            term = w_ref[n_taps - 1 - s:n_taps - s, :] * src
            acc = term if acc is None else acc + term
        return acc

    def tiles(body):
        def step(i, carry):
            r = i // (cls_rows // CONV_ROWS)
            row0 = (i % (cls_rows // CONV_ROWS)) * CONV_ROWS
            body(r, row0, pl.ds(pl.multiple_of(r * cls_rows + row0, CONV_ROWS), CONV_ROWS))
            return carry
        lax.fori_loop(0, N_CLASSES * (cls_rows // CONV_ROWS), step, 0)

    fill(lambda rows: z_ref[rows, g1:g2],
         (GCONV_K - 1) // N_CLASSES + 2)

    def gconv(r, row0, rows):
        acc = conv_tile(gw_ref, GCONV_K, r, row0)
        o_ref[rows, 0:GCONV_WIDTH] = (z_ref[rows, :g1] * acc).astype(o_ref.dtype)

    tiles(gconv)

    fill(lambda rows: z_ref[rows, g2:], CONV_SHIFTS)

    def conv_class(r):
        for row0 in range(0, cls_rows, CONV_ROWS):
            rows = pl.ds(pl.multiple_of(r * cls_rows + row0, CONV_ROWS), CONV_ROWS)
            pre_ref[rows, :] = conv_tile(cw_ref, CCONV_K, r, row0) + cb_ref[...]

    def norm_class(r):
        rows = pl.ds(pl.multiple_of(r * cls_rows, cls_rows), cls_rows)
        u = pre_ref[rows, :]
        mu = jnp.mean(u, axis=-1, keepdims=True)
        d = u - mu
        var = jnp.mean(d * d, axis=-1, keepdims=True)
        y = d * lax.rsqrt(var + LN_EPS) * lg_ref[...] + lb_ref[...]
        o_ref[rows, GCONV_WIDTH:] = (y * jax.nn.sigmoid(y)).astype(o_ref.dtype)

    conv_class(0)

    def step(r, carry):
        norm_class(r - 1)
        conv_class(r)
        return carry

    lax.fori_loop(1, N_CLASSES, step, 0)
    norm_class(N_CLASSES - 1)


def _conv_mixers(rest, gconv_w, cconv_w, cconv_b, cln_g, cln_b, batch, seq):
    width = GCONV_WIDTH + CCONV_WIDTH
    return pl.pallas_call(
        _conv_kernel,
        grid=(batch,),
        in_specs=[pl.BlockSpec((seq, CONV_IN_WIDTH), lambda b: (b, 0)),
                  _resident((GCONV_K, GCONV_WIDTH)), _resident((CCONV_K, CCONV_WIDTH)),
                  _resident((1, CCONV_WIDTH)), _resident((1, CCONV_WIDTH)),
                  _resident((1, CCONV_WIDTH))],
        out_specs=pl.BlockSpec((seq, width), lambda b: (b, 0)),
        out_shape=jax.ShapeDtypeStruct((batch * seq, width), BF16),
        scratch_shapes=[pltpu.VMEM((CONV_SHIFTS, seq, GCONV_WIDTH), F32),
                        pltpu.VMEM((seq, CCONV_WIDTH), F32)],
        compiler_params=_params(1), name="conv_mixers")(
            rest, gconv_w, cconv_w, cconv_b.reshape(1, -1), cln_g.reshape(1, -1),
            cln_b.reshape(1, -1))


def kernel(x, w_in, w_out, gconv_w, cconv_w, cconv_b, cln_g, cln_b, ffn1_wg, ffn1_wu, ffn1_wd, ffn2_wg, ffn2_wu, ffn2_wd, norm_ffn1, norm_mix, norm_ffn2, norm_final):
    batch, seq, _ = x.shape
    depth = w_in.shape[0]
    assert seq == N_CLASSES * BLK and x.shape[2] == D_MODEL
    bias_prev, bias_first = (jnp.asarray(t) for t in _alibi_bias_tables())
    ffn1 = [w.astype(BF16) for w in (ffn1_wg, ffn1_wu, ffn1_wd)]
    ffn2 = [w.astype(BF16) for w in (ffn2_wg, ffn2_wu, ffn2_wd)]
    w_in, w_out = w_in.astype(BF16), w_out.astype(BF16)
    h = x
    for l in range(depth):
        last = l == depth - 1
        h = _ffn(h, l, norm_ffn1[l], *ffn1, natural_in=(batch, seq) if l == 0 else None)
        qkv, rest = _inproj(h, l, norm_mix[l], w_in)
        y_attn = _attention(qkv, bias_prev, bias_first, batch, seq)
        y_conv = _conv_mixers(rest, gconv_w[l], cconv_w[l], cconv_b[l], cln_g[l],
                              cln_b[l], batch, seq)
        h = _ffn(h, l, norm_ffn2[l], *ffn2, final_gain=norm_final if last else None,
                 natural_out=(batch, seq) if last else None,
                 mixed=(y_attn, y_conv, w_out))
    return h
```

```python
import functools
import math

import numpy as np
import jax
import jax.numpy as jnp
from jax import lax
from jax.experimental import pallas as pl
from jax.experimental.pallas import tpu as pltpu

D_MODEL = 1024
D_FF = 11 * D_MODEL // 4
HEAD_DIM = 64
ATTN_WIDTH = D_MODEL // 2
N_HEADS = ATTN_WIDTH // HEAD_DIM
GCONV_WIDTH = D_MODEL // 4
CCONV_WIDTH = D_MODEL // 4
GCONV_K = 3
CCONV_K = 31
QKV_WIDTH = 3 * ATTN_WIDTH
REST_WIDTH = 3 * GCONV_WIDTH + 2 * CCONV_WIDTH
D_IN = QKV_WIDTH + REST_WIDTH
CONV_IN_WIDTH = 2 * GCONV_WIDTH + CCONV_WIDTH
DILATIONS = (1, 4, 16)
N_STEPS = 128
BLK = 128
N_CLASSES = max(DILATIONS)
LANES = 128
SUBLANES = 8
NATURAL_CLASSES = SUBLANES
FFN_ROWS = 256
PROJ_ROWS = 256
FFN_BLOCK = NATURAL_CLASSES * BLK
RMS_EPS = 1e-6
LN_EPS = 1e-5
LOG2E = math.log2(math.e)
ATTN_UNROLL = 2
PAIRS_PER_STEP = 2
COMBINE_CHUNK = 256
CONV_ROWS = 64
CONV_SHIFTS = (CCONV_K - 1) // N_CLASSES + 2
VMEM_LIMIT = 56 * 1024 * 1024

F32 = jnp.float32
BF16 = jnp.bfloat16


def _rms(x, g):
    return x * lax.rsqrt(jnp.mean(x * x, axis=-1, keepdims=True) + RMS_EPS) * g


def _params(n_axes):
    return pltpu.CompilerParams(
        dimension_semantics=("arbitrary",) * n_axes, vmem_limit_bytes=VMEM_LIMIT)


def _resident(shape, layer=None):
    if layer is None:
        return pl.BlockSpec(shape, lambda *_: (0,) * len(shape),
                            pipeline_mode=pl.Buffered(1))
    return pl.BlockSpec((None,) + shape, lambda *_: (layer,) + (0,) * len(shape),
                        pipeline_mode=pl.Buffered(1))


def _ffn_kernel(x_ref, g_ref, wg_ref, wu_ref, wd_ref, *rest, final, natural_in,
                natural_out, project):
    o_ref = rest[-1]
    cls_per_sub = FFN_ROWS // BLK

    def prepare(i):
        rows = pl.ds(i * FFN_ROWS, FFN_ROWS)
        if natural_in:
            x = jnp.concatenate(
                [x_ref[:, i * cls_per_sub + c, :] for c in range(cls_per_sub)], axis=0)
        else:
            x = x_ref[rows, :]
        if project:
            ya_ref, yc_ref, wo_ref = rest[:3]
            x = (x + jnp.dot(ya_ref[rows, :], wo_ref[:ATTN_WIDTH, :],
                             preferred_element_type=F32)
                 + jnp.dot(yc_ref[rows, :], wo_ref[ATTN_WIDTH:, :],
                           preferred_element_type=F32))
        return x, _rms(x, g_ref[...]).astype(BF16)

    def finish(i, x, h):
        gate = jnp.dot(h, wg_ref[...], preferred_element_type=F32)
        up = jnp.dot(h, wu_ref[...], preferred_element_type=F32)
        act = (gate * jax.nn.sigmoid(gate) * up).astype(BF16)
        y = x + 0.5 * jnp.dot(act, wd_ref[...], preferred_element_type=F32)
        if final:
            y = _rms(y, rest[-2][...])
        if natural_out:
            for c in range(cls_per_sub):
                o_ref[:, i * cls_per_sub + c, :] = y[c * BLK:(c + 1) * BLK]
        else:
            o_ref[pl.ds(i * FFN_ROWS, FFN_ROWS), :] = y

    n_sub = (x_ref.shape[1] * BLK if natural_in else x_ref.shape[0]) // FFN_ROWS
    prepared = [prepare(i) for i in range(n_sub)]
    for i in range(n_sub):
        finish(i, *prepared[i])


def _ffn(x, layer, gain, wg, wu, wd, final_gain=None, natural_in=None, natural_out=None,
         mixed=None):
    natural = natural_in or natural_out
    if natural:
        batch, seq = natural
        assert seq == N_CLASSES * BLK
        t = batch * seq
        halves = N_CLASSES // NATURAL_CLASSES
        nat_block = pl.BlockSpec((None, BLK, NATURAL_CLASSES, D_MODEL),
                                 lambda b, j: (b, 0, j, 0))
        cls_block = pl.BlockSpec((FFN_BLOCK, D_MODEL),
                                 lambda b, j: (b * halves + j, 0))
        grid = (batch, halves)
        x_spec = nat_block if natural_in else cls_block
        out_spec = nat_block if natural_out else cls_block
        y_spec = pl.BlockSpec((FFN_BLOCK, ATTN_WIDTH),
                              lambda b, j: (b * halves + j, 0))
        if natural_in:
            x = x.reshape(batch, BLK, N_CLASSES, D_MODEL)
    else:
        t = x.shape[0]
        grid = (t // FFN_BLOCK,)
        x_spec = out_spec = pl.BlockSpec((FFN_BLOCK, D_MODEL), lambda i: (i, 0))
        y_spec = pl.BlockSpec((FFN_BLOCK, ATTN_WIDTH), lambda i: (i, 0))
    out_shape = ((batch, BLK, N_CLASSES, D_MODEL) if natural_out else (t, D_MODEL))
    in_specs = [x_spec, _resident((1, D_MODEL)), _resident((D_MODEL, D_FF), layer),
                _resident((D_MODEL, D_FF), layer), _resident((D_FF, D_MODEL), layer)]
    args = [x, gain.reshape(1, D_MODEL), wg, wu, wd]
    if mixed is not None:
        in_specs += [y_spec, y_spec, _resident((D_MODEL, D_MODEL), layer)]
        args += list(mixed)
    if final_gain is not None:
        in_specs.append(_resident((1, D_MODEL)))
        args.append(final_gain.reshape(1, D_MODEL))
    out = pl.pallas_call(
        functools.partial(_ffn_kernel, final=final_gain is not None,
                          natural_in=bool(natural_in), natural_out=bool(natural_out),
                          project=mixed is not None),
        grid=grid, in_specs=in_specs, out_specs=out_spec,
        out_shape=jax.ShapeDtypeStruct(out_shape, F32),
        compiler_params=_params(len(grid)), name="ffn")(*args)
    return out.reshape(batch, seq, D_MODEL) if natural_out else out


def _inproj_kernel(x_ref, g_ref, w_ref, qkv_ref, conv_ref):
    g1, g2 = GCONV_WIDTH, 2 * GCONV_WIDTH
    c0 = 3 * GCONV_WIDTH
    c1 = c0 + CCONV_WIDTH
    n_sub = x_ref.shape[0] // PROJ_ROWS
    hs = [_rms(x_ref[i * PROJ_ROWS:(i + 1) * PROJ_ROWS, :], g_ref[...]).astype(BF16)
          for i in range(n_sub)]
    for i, h in enumerate(hs):
        rows = slice(i * PROJ_ROWS, (i + 1) * PROJ_ROWS)
        qkv_ref[rows, :] = jnp.dot(h, w_ref[:, :QKV_WIDTH], preferred_element_type=F32)
        z = jnp.dot(h, w_ref[:, QKV_WIDTH:], preferred_element_type=F32)
        conv_ref[rows, :g1] = z[:, :g1]
        conv_ref[rows, g1:g2] = z[:, g1:g2] * z[:, g2:c0]
        conv_ref[rows, g2:] = z[:, c0:c1] * jax.nn.sigmoid(z[:, c1:])


def _inproj(x, layer, gain, w_in, tm=1024):
    t = x.shape[0]
    return pl.pallas_call(
        _inproj_kernel, grid=(t // tm,),
        in_specs=[pl.BlockSpec((tm, D_MODEL), lambda i: (i, 0)),
                  _resident((1, D_MODEL)), _resident((D_MODEL, D_IN), layer)],
        out_specs=[pl.BlockSpec((tm, QKV_WIDTH), lambda i: (i, 0)),
                   pl.BlockSpec((tm, CONV_IN_WIDTH), lambda i: (i, 0))],
        out_shape=[jax.ShapeDtypeStruct((t, QKV_WIDTH), F32),
                   jax.ShapeDtypeStruct((t, CONV_IN_WIDTH), F32)],
        compiler_params=_params(1), name="inproj")(x, gain.reshape(1, D_MODEL), w_in)


def _alibi_bias_tables():
    slopes = np.exp2(-8.0 * np.arange(1, N_HEADS + 1, dtype=np.float64) / N_HEADS)
    with_prev, first = [], []
    for dil in DILATIONS:
        groups = N_CLASSES // dil
        piece = BLK // groups
        q_pos = (groups * np.arange(piece)[None, :] + np.arange(groups)[:, None]).reshape(-1)
        for table, key_piece, shift in ((with_prev, 2 * piece, BLK), (first, piece, 0)):
            k_pos = (groups * np.arange(key_piece)[None, :]
                     + np.arange(groups)[:, None]).reshape(-1) - shift
            steps = q_pos[:, None] - k_pos[None, :]
            valid = (steps >= 0) & (steps <= N_STEPS)
            bias = -slopes[:, None, None] * (steps * dil)[None] * LOG2E
            table.append(np.where(valid[None], bias, -np.inf).astype(np.float32))
    return np.stack(with_prev), np.stack(first)


def _attn_kernel(q_ref, k_ref, v_ref, bias_prev_ref, bias_first_ref, o_ref,
                 acc_ref, max_ref, s_ref, p_ref):
    for pair in range(PAIRS_PER_STEP):
        lanes = pl.ds(pair * LANES, LANES)
        heads = pl.ds(2 * pair, 2)
        _attend_pair(q_ref.at[:, lanes], k_ref.at[:, lanes], v_ref.at[:, lanes],
                     bias_prev_ref.at[:, heads], bias_first_ref.at[:, heads],
                     o_ref.at[:, lanes], acc_ref, max_ref, s_ref, p_ref)


def _attend_pair(q_ref, k_ref, v_ref, bias_prev_ref, bias_first_ref, o_ref,
                 acc_ref, max_ref, s_ref, p_ref):
    lane = lax.broadcasted_iota(jnp.int32, (1, LANES), 1)
    head_mask = (lane < HEAD_DIM, lane >= HEAD_DIM)

    def pieces(dil, cls, n, with_prev):
        groups = N_CLASSES // dil
        piece = BLK // groups
        first = cls * BLK + (n - 1 if with_prev else n) * piece
        rows = 2 * piece if with_prev else piece
        if not isinstance(first, int):
            first = pl.multiple_of(first, SUBLANES)
        return [((dil * g) * BLK + first, rows) for g in range(groups)]

    def gather(ref, where):
        parts = [ref[pl.ds(start, rows), :] for start, rows in where]
        return parts[0] if len(parts) == 1 else jnp.concatenate(parts, axis=0)

    def scatter(ref, slab, where, value):
        offset = 0
        for start, rows in where:
            ref[slab, pl.ds(start, rows), :] = value[offset:offset + rows]
            offset += rows

    def scores(slot, bi, dil, blocks):
        for u, (cls, n, has_prev) in enumerate(blocks):
            keys = 2 * BLK if has_prev else BLK
            q = gather(q_ref, pieces(dil, cls, n, False)) * (HEAD_DIM ** -0.5 * LOG2E)
            kb = gather(k_ref, pieces(dil, cls, n, has_prev)).astype(BF16)
            for h in range(2):
                qh = jnp.where(head_mask[h], q, 0.0).astype(BF16)
                s = lax.dot_general(qh, kb, (((1,), (1,)), ((), ())),
                                    preferred_element_type=F32)
                bias = bias_prev_ref[bi, h] if has_prev else bias_first_ref[bi, h]
                s_ref[slot, 2 * u + h, :, :keys] = s + bias

    def softmax(slot, bi, dil, blocks):
        for u, (cls, n, has_prev) in enumerate(blocks):
            keys = 2 * BLK if has_prev else BLK
            for h in range(2):
                s = s_ref[slot, 2 * u + h, :, :keys]
                m = jnp.max(s, axis=1, keepdims=True)
                p_ref[slot, 2 * u + h, :, :keys] = jnp.exp2(s - m).astype(BF16)
                scatter(max_ref, 2 * bi + h, pieces(dil, cls, n, False),
                        jnp.broadcast_to(m, (BLK, LANES)))

    def weighted_values(slot, bi, dil, blocks):
        for u, (cls, n, has_prev) in enumerate(blocks):
            keys = 2 * BLK if has_prev else BLK
            v = gather(v_ref, pieces(dil, cls, n, has_prev))
            for h in range(2):
                v_aug = jnp.where(head_mask[h], v, 1.0).astype(BF16)
                scatter(acc_ref, 2 * bi + h, pieces(dil, cls, n, False),
                        jnp.dot(p_ref[slot, 2 * u + h, :, :keys], v_aug,
                                preferred_element_type=F32))

    groups = []
    for bi, dil in enumerate(DILATIONS):
        blocks = [(cls, n, n > 0) for cls in range(dil) for n in range(N_CLASSES // dil)]
        size = ATTN_UNROLL * (2 if N_CLASSES // dil == 1 else 1)
        groups += [(bi, dil, blocks[i:i + size]) for i in range(0, len(blocks), size)]
    for i in range(len(groups) + 2):
        if i < len(groups):
            scores(i % 2, *groups[i])
        if 1 <= i <= len(groups):
            softmax((i - 1) % 2, *groups[i - 1])
        if i >= 2:
            weighted_values(i % 2, *groups[i - 2])

    for c in range(o_ref.shape[0] // COMBINE_CHUNK):
        rows = pl.ds(c * COMBINE_CHUNK, COMBINE_CHUNK)
        tot = []
        for h in range(2):
            ms = [max_ref[2 * bi + h, rows, :] for bi in range(len(DILATIONS))]
            m = jnp.maximum(jnp.maximum(ms[0], ms[1]), ms[2])
            tot.append(sum(jnp.exp2(ms[bi] - m) * acc_ref[2 * bi + h, rows, :]
                           for bi in range(len(DILATIONS))))
        num = jnp.where(head_mask[0], tot[0], tot[1])
        den = pltpu.roll(jnp.where(head_mask[0], tot[1], tot[0]), HEAD_DIM, axis=1)
        o_ref[rows, :] = (num / den).astype(o_ref.dtype)


def _attention(qkv, bias_prev, bias_first, batch, seq):
    steps = N_HEADS // 2 // PAIRS_PER_STEP
    width = PAIRS_PER_STEP * LANES
    n_slabs = 2 * len(DILATIONS)
    n_dil = len(DILATIONS)
    return pl.pallas_call(
        _attn_kernel,
        grid=(batch, steps),
        in_specs=[pl.BlockSpec((seq, width), lambda b, p: (b, p)),
                  pl.BlockSpec((seq, width), lambda b, p: (b, steps + p)),
                  pl.BlockSpec((seq, width), lambda b, p: (b, 2 * steps + p)),
                  pl.BlockSpec((n_dil, 2 * PAIRS_PER_STEP, BLK, 2 * BLK),
                               lambda b, p: (0, p, 0, 0)),
                  pl.BlockSpec((n_dil, 2 * PAIRS_PER_STEP, BLK, BLK),
                               lambda b, p: (0, p, 0, 0))],
        out_specs=pl.BlockSpec((seq, width), lambda b, p: (b, p)),
        out_shape=jax.ShapeDtypeStruct((batch * seq, ATTN_WIDTH), BF16),
        scratch_shapes=[pltpu.VMEM((n_slabs, seq, LANES), F32),
                        pltpu.VMEM((n_slabs, seq, LANES), F32),
                        pltpu.VMEM((2, 4 * ATTN_UNROLL, BLK, 2 * BLK), F32),
                        pltpu.VMEM((2, 4 * ATTN_UNROLL, BLK, 2 * BLK), BF16)],
        compiler_params=_params(2), name="dilated_attn")(qkv, qkv, qkv, bias_prev, bias_first)


def _conv_kernel(z_ref, gw_ref, cw_ref, cb_ref, lg_ref, lb_ref, o_ref, u_ref, pre_ref):
    cls_rows = z_ref.shape[0] // N_CLASSES
    g1, g2 = GCONV_WIDTH, 2 * GCONV_WIDTH
    row = lax.broadcasted_iota(jnp.int32, (cls_rows, 1), 0)

    def fill(conv_input, n_shifts):
        def body(r, carry):
            rows = pl.ds(pl.multiple_of(r * cls_rows, cls_rows), cls_rows)
            u = conv_input(rows)
            u_ref[0, rows, :] = u
            for d in range(1, n_shifts):
                u_ref[d, rows, :] = jnp.where(row >= d, pltpu.roll(u, d, axis=0), 0.0)
            return carry
        lax.fori_loop(0, N_CLASSES, body, 0)

    def conv_tile(w_ref, n_taps, r, row0):
        acc = None
        for s in range(n_taps):
            whole, part = divmod(s, N_CLASSES)
            borrow = jnp.where(r < part, 1, 0) if part else 0
            src_cls = r - part + N_CLASSES * borrow
            start = pl.multiple_of(src_cls * cls_rows + row0, CONV_ROWS)
            src = u_ref[whole + borrow, pl.ds(start, CONV_ROWS), :]
            term = w_ref[n_taps - 1 - s:n_taps - s, :] * src
            acc = term if acc is None else acc + term
        return acc

    def tiles(body):
        def step(i, carry):
            r = i // (cls_rows // CONV_ROWS)
            row0 = (i % (cls_rows // CONV_ROWS)) * CONV_ROWS
            body(r, row0, pl.ds(pl.multiple_of(r * cls_rows + row0, CONV_ROWS), CONV_ROWS))
            return carry
        lax.fori_loop(0, N_CLASSES * (cls_rows // CONV_ROWS), step, 0)

    fill(lambda rows: z_ref[rows, g1:g2],
         (GCONV_K - 1) // N_CLASSES + 2)

    def gconv(r, row0, rows):
        acc = conv_tile(gw_ref, GCONV_K, r, row0)
        o_ref[rows, 0:GCONV_WIDTH] = (z_ref[rows, :g1] * acc).astype(o_ref.dtype)

    tiles(gconv)

    fill(lambda rows: z_ref[rows, g2:], CONV_SHIFTS)

    def conv_class(r):
        for row0 in range(0, cls_rows, CONV_ROWS):
            rows = pl.ds(pl.multiple_of(r * cls_rows + row0, CONV_ROWS), CONV_ROWS)
            pre_ref[rows, :] = conv_tile(cw_ref, CCONV_K, r, row0) + cb_ref[...]

    def norm_class(r):
        rows = pl.ds(pl.multiple_of(r * cls_rows, cls_rows), cls_rows)
        u = pre_ref[rows, :]
        mu = jnp.mean(u, axis=-1, keepdims=True)
        d = u - mu
        var = jnp.mean(d * d, axis=-1, keepdims=True)
        y = d * lax.rsqrt(var + LN_EPS) * lg_ref[...] + lb_ref[...]
        o_ref[rows, GCONV_WIDTH:] = (y * jax.nn.sigmoid(y)).astype(o_ref.dtype)

    conv_class(0)

    def step(r, carry):
        norm_class(r - 1)
        conv_class(r)
        return carry

    lax.fori_loop(1, N_CLASSES, step, 0)
    norm_class(N_CLASSES - 1)


def _conv_mixers(rest, gconv_w, cconv_w, cconv_b, cln_g, cln_b, batch, seq):
    width = GCONV_WIDTH + CCONV_WIDTH
    return pl.pallas_call(
        _conv_kernel,
        grid=(batch,),
        in_specs=[pl.BlockSpec((seq, CONV_IN_WIDTH), lambda b: (b, 0)),
                  _resident((GCONV_K, GCONV_WIDTH)), _resident((CCONV_K, CCONV_WIDTH)),
                  _resident((1, CCONV_WIDTH)), _resident((1, CCONV_WIDTH)),
                  _resident((1, CCONV_WIDTH))],
        out_specs=pl.BlockSpec((seq, width), lambda b: (b, 0)),
        out_shape=jax.ShapeDtypeStruct((batch * seq, width), BF16),
        scratch_shapes=[pltpu.VMEM((CONV_SHIFTS, seq, GCONV_WIDTH), F32),
                        pltpu.VMEM((seq, CCONV_WIDTH), F32)],
        compiler_params=_params(1), name="conv_mixers")(
            rest, gconv_w, cconv_w, cconv_b.reshape(1, -1), cln_g.reshape(1, -1),
            cln_b.reshape(1, -1))


def kernel(x, w_in, w_out, gconv_w, cconv_w, cconv_b, cln_g, cln_b, ffn1_wg, ffn1_wu, ffn1_wd, ffn2_wg, ffn2_wu, ffn2_wd, norm_ffn1, norm_mix, norm_ffn2, norm_final):
    batch, seq, _ = x.shape
    depth = w_in.shape[0]
    assert seq == N_CLASSES * BLK and x.shape[2] == D_MODEL
    bias_prev, bias_first = (jnp.asarray(t) for t in _alibi_bias_tables())
    ffn1 = [w.astype(BF16) for w in (ffn1_wg, ffn1_wu, ffn1_wd)]
    ffn2 = [w.astype(BF16) for w in (ffn2_wg, ffn2_wu, ffn2_wd)]
    w_in, w_out = w_in.astype(BF16), w_out.astype(BF16)
    h = x
    for l in range(depth):
        last = l == depth - 1
        h = _ffn(h, l, norm_ffn1[l], *ffn1, natural_in=(batch, seq) if l == 0 else None)
        qkv, rest = _inproj(h, l, norm_mix[l], w_in)
        y_attn = _attention(qkv, bias_prev, bias_first, batch, seq)
        y_conv = _conv_mixers(rest, gconv_w[l], cconv_w[l], cconv_b[l], cln_g[l],
                              cln_b[l], batch, seq)
        h = _ffn(h, l, norm_ffn2[l], *ffn2, final_gain=norm_final if last else None,
                 natural_out=(batch, seq) if last else None,
                 mixed=(y_attn, y_conv, w_out))
    return h
```
